```python
import math
import jax, jax.numpy as jnp
from jax import lax
import numpy as np

D_MODEL = 2048
BATCH = 8
SEQ = 2048
DEPTH = 1
DEC_BATCH = 128
DEC_SEQ = 4
PAST_LEN = 2048
PAGE_SIZE = 128

RET_HEADS = 8
RET_DK = D_MODEL // 16
RET_DV = 2 * RET_DK
RET_CHUNK = 128
ROPE_BASE = 10000.0
DIFF_HEADS = 8
DIFF_HD = D_MODEL // 32
Q_BLOCK = 128
REL_BUCKETS = 32
REL_MAX_DIST = 128
MEM_TOKENS = 256
MEM_HEADS = 4
MEM_HD = D_MODEL // 16
PEER_HEADS = 8
PEER_NKEYS = 128
PEER_EXPERTS = PEER_NKEYS * PEER_NKEYS
PEER_TOPK = 16
PEER_DK = 128
PEER_BLOCK = 128
EPS = 1e-6

RET_QK_W = RET_HEADS * RET_DK
RET_V_W = RET_HEADS * RET_DV
DIFF_QK_W = 2 * DIFF_HEADS * DIFF_HD
DIFF_V_W = DIFF_HEADS * 2 * DIFF_HD
MEM_W = MEM_HEADS * MEM_HD
IN_SPLITS = (RET_QK_W, RET_QK_W, RET_V_W, RET_V_W, DIFF_QK_W, DIFF_QK_W, DIFF_V_W, MEM_W, D_MODEL, D_MODEL, D_MODEL)
IN_WIDTH = sum(IN_SPLITS)

kernel_name = 'hybrid_retention_diffattn_peer_decode_step'


def lambda_init(layer):
    return 0.8 - 0.6 * math.exp(-0.3 * layer)


def rmsnorm(x, w=None):
    xf = x.astype(jnp.float32)
    y = xf * lax.rsqrt(jnp.mean(xf * xf, axis=-1, keepdims=True) + EPS)
    if w is not None:
        y = y * w.astype(jnp.float32)
    return y.astype(x.dtype)


def rotary(x, pos):
    half = x.shape[-1] // 2
    inv = 1.0 / (ROPE_BASE ** jnp.linspace(0.0, 1.0, half, dtype=jnp.float32))
    ang = pos.astype(jnp.float32)[:, None] * inv[None, :]
    c = jnp.cos(ang)[None, :, None, :]
    s = jnp.sin(ang)[None, :, None, :]
    xf = x.astype(jnp.float32)
    x1, x2 = xf[..., :half], xf[..., half:]
    return jnp.concatenate([x1 * c - x2 * s, x1 * s + x2 * c], axis=-1)


def ret_log_gamma():
    return jnp.log1p(-jnp.exp2(-5.0 - jnp.arange(RET_HEADS, dtype=jnp.float32)))


def retention_chunk(state, q, k, v):
    c_len = q.shape[2]
    lg = ret_log_gamma()[:, None]
    idx = jnp.arange(c_len, dtype=jnp.float32)
    rel = idx[:, None] - idx[None, :]
    dmask = jnp.where(rel >= 0, jnp.exp(lg[:, :, None] * jnp.maximum(rel, 0.0)), 0.0)
    qk = jnp.einsum('bhid,bhjd->bhij', q, k) * dmask[None]
    inner = jnp.einsum('bhij,bhje->bhie', qk, v)
    cross = jnp.einsum('bhid,bhde->bhie', q, state) * jnp.exp(lg * (idx + 1.0))[None, :, :, None]
    k_dec = k * jnp.exp(lg * (c_len - 1.0 - idx))[None, :, :, None]
    new_state = jnp.exp(lg * c_len)[None, :, :, None] * state + jnp.einsum('bhjd,bhje->bhde', k_dec, v)
    return new_state, inner + cross


def retention(q, k, v, state0):
    b, s, h, _ = q.shape
    c_len = RET_CHUNK if s % RET_CHUNK == 0 else s
    n_chunks = s // c_len

    def to_chunks(t):
        return t.astype(jnp.float32).reshape(b, n_chunks, c_len, h, t.shape[-1]).transpose(1, 0, 3, 2, 4)

    def step(st, qkv):
        return retention_chunk(st, *qkv)

    state_fin, o = lax.scan(step, state0.astype(jnp.float32), (to_chunks(q), to_chunks(k), to_chunks(v)))
    o = o.transpose(1, 0, 3, 2, 4).reshape(b, s, h, -1)
    return o, state_fin


def rel_bucket(rel):
    n = jnp.maximum(rel, 0)
    max_exact = REL_BUCKETS // 2
    nf = jnp.maximum(n, 1).astype(jnp.float32)
    large = max_exact + (jnp.log(nf / max_exact) / math.log(REL_MAX_DIST / max_exact)
                         * (REL_BUCKETS - max_exact)).astype(jnp.int32)
    large = jnp.minimum(large, REL_BUCKETS - 1)
    return jnp.where(n < max_exact, n, large)


def diff_attention(q, k, v, q_pos, k_pos, lam, lam_init, rel_bias, subln_w):
    b, nq = q.shape[:2]
    s = jnp.einsum('bqhd,bkhd->bhqk', q, k).astype(jnp.float32) * (DIFF_HD ** -0.5)
    s = s.reshape(b, DIFF_HEADS, 2, nq, k.shape[1])
    rel = q_pos[:, None] - k_pos[None, :]
    bias = rel_bias.astype(jnp.float32)[rel_bucket(rel)].transpose(2, 0, 1)
    s = jnp.where(rel >= 0, s + bias[None, :, None], -jnp.inf)
    p = jax.nn.softmax(s, axis=-1)
    a = p[:, :, 0] - lam * p[:, :, 1]
    o = jnp.einsum('bhqk,bkhe->bqhe', a, v.astype(jnp.float32))
    o = rmsnorm(o, subln_w) * (1.0 - lam_init)
    return o.reshape(b, nq, DIFF_V_W)


def mem_attention(q, mk, mv):
    b, nq = q.shape[:2]
    s = jnp.einsum('bqhd,bkhd->bhqk', q, mk).astype(jnp.float32) * (MEM_HD ** -0.5)
    p = jax.nn.softmax(s, axis=-1)
    o = jnp.einsum('bhqk,bkhd->bqhd', p, mv.astype(jnp.float32))
    return o.reshape(b, nq, MEM_W)


def peer_ffn(xn, w_q, keys, u, v):
    b, s, d = xn.shape
    t = xn.reshape(b * s, d)
    n = t.shape[0]
    t = jnp.pad(t, ((0, (-n) % PEER_BLOCK), (0, 0)))

    def block(tb):
        nt = tb.shape[0]
        q = (tb @ w_q).reshape(nt, PEER_HEADS, 2, PEER_DK)
        sc = jnp.einsum('thcd,hcnd->thcn', q, keys).astype(jnp.float32)
        s1, i1 = lax.top_k(sc[:, :, 0], PEER_TOPK)
        s2, i2 = lax.top_k(sc[:, :, 1], PEER_TOPK)
        cand_s = (s1[..., :, None] + s2[..., None, :]).reshape(nt, PEER_HEADS, -1)
        cand_i = (i1[..., :, None] * PEER_NKEYS + i2[..., None, :]).reshape(nt, PEER_HEADS, -1)
        top_s, j = lax.top_k(cand_s, PEER_TOPK)
        eid = jnp.take_along_axis(cand_i, j, axis=-1)
        g = jax.nn.softmax(top_s, axis=-1)
        act = jax.nn.gelu(jnp.einsum('thkd,td->thk', u[eid], tb).astype(jnp.float32), approximate=False)
        return jnp.einsum('thk,thkd->td', (g * act).astype(v.dtype), v[eid])

    out = lax.map(block, t.reshape(-1, PEER_BLOCK, d))
    return out.reshape(-1, d)[:n].reshape(b, s, d).astype(xn.dtype)


def decoder_layer(x, pos, ret_state0, mem_k, mem_v, diff_attend, lp):
    b, s, _ = x.shape
    xn = rmsnorm(x, lp['norm_mix_w'])
    z = xn @ lp['w_in']
    offs = [int(o) for o in np.cumsum(IN_SPLITS)[:-1]]
    rq, rk, rv, rg, dq, dk, dv, mq, ga, gb, gc = jnp.split(z, offs, axis=-1)
    rq = rotary(rq.reshape(b, s, RET_HEADS, RET_DK), pos)
    rk = rotary(rk.reshape(b, s, RET_HEADS, RET_DK), pos) * (RET_DK ** -0.5)
    o_r, ret_state = retention(rq, rk, rv.reshape(b, s, RET_HEADS, RET_DV), ret_state0)
    o_r = rmsnorm(o_r).reshape(b, s, RET_V_W) * jax.nn.silu(rg.astype(jnp.float32))
    dk = dk.reshape(b, s, 2 * DIFF_HEADS, DIFF_HD)
    dv = dv.reshape(b, s, DIFF_HEADS, 2 * DIFF_HD)
    o_d = diff_attend(dq.reshape(b, s, 2 * DIFF_HEADS, DIFF_HD), dk, dv)
    o_m = mem_attention(mq.reshape(b, s, MEM_HEADS, MEM_HD), mem_k, mem_v)
    dt = x.dtype
    mixed = (jax.nn.sigmoid(ga.astype(jnp.float32)) * (o_r.astype(dt) @ lp['w_up_ret'])
             + jax.nn.sigmoid(gb.astype(jnp.float32)) * (o_d.astype(dt) @ lp['w_up_diff'])
             + jax.nn.sigmoid(gc.astype(jnp.float32)) * (o_m.astype(dt) @ lp['w_up_mem']))
    h = x + (mixed.astype(dt) @ lp['w_out']).astype(dt)
    h = h + peer_ffn(rmsnorm(h, lp['norm_ffn_w']), lp['peer_w_q'], lp['peer_keys'], lp['peer_u'], lp['peer_v'])
    return h, ret_state, dk, dv


def setup_inputs(seed: int = 0) -> dict:
    key = jax.random.key(seed)
    ks = jax.random.split(key, 32)
    n_pages = PAST_LEN // PAGE_SIZE
    n_phys = (5 * DEC_BATCH * n_pages + 3) // 4
    f32 = jnp.float32

    def nrm(k, shape, scale):
        return jax.random.normal(k, shape, f32) * scale

    def gain(k, shape):
        return 1.0 + 0.05 * jax.random.normal(k, shape, f32)

    page_table = jax.random.permutation(ks[8], n_phys)[: DEC_BATCH * n_pages].reshape(DEC_BATCH, n_pages).astype(jnp.int32)
    return {
        'x_prompt': nrm(ks[0], (BATCH, SEQ, D_MODEL), 1.0),
        'x_sample': nrm(ks[1], (DEC_BATCH, DEC_SEQ, D_MODEL), 1.0),
        'mem_prompt': nrm(ks[2], (BATCH, MEM_TOKENS, D_MODEL), 1.0),
        'state_ret': nrm(ks[3], (DEPTH, DEC_BATCH, RET_HEADS, RET_DK, RET_DV), 0.5),
        'cache_diff_k': nrm(ks[4], (DEPTH, n_phys, PAGE_SIZE, 2 * DIFF_HEADS, DIFF_HD), 1.0),
        'cache_diff_v': nrm(ks[5], (DEPTH, n_phys, PAGE_SIZE, DIFF_HEADS, 2 * DIFF_HD), 1.0),
        'cache_mem_k': nrm(ks[6], (DEPTH, DEC_BATCH, MEM_TOKENS, MEM_HEADS, MEM_HD), 1.0),
        'cache_mem_v': nrm(ks[7], (DEPTH, DEC_BATCH, MEM_TOKENS, MEM_HEADS, MEM_HD), 1.0),
        'page_table': page_table,
        'norm_mix_w': gain(ks[9], (DEPTH, D_MODEL)),
        'norm_mem_w': gain(ks[10], (DEPTH, D_MODEL)),
        'norm_ffn_w': gain(ks[11], (DEPTH, D_MODEL)),
        'norm_final_w': gain(ks[12], (D_MODEL,)),
        'w_in': nrm(ks[13], (DEPTH, D_MODEL, IN_WIDTH), D_MODEL ** -0.5),
        'w_mem_k': nrm(ks[14], (DEPTH, D_MODEL, MEM_W), D_MODEL ** -0.5),
        'w_mem_v': nrm(ks[15], (DEPTH, D_MODEL, MEM_W), D_MODEL ** -0.5),
        'diff_lambda_q1': nrm(ks[16], (DEPTH, DIFF_HD), 0.1),
        'diff_lambda_k1': nrm(ks[17], (DEPTH, DIFF_HD), 0.1),
        'diff_lambda_q2': nrm(ks[18], (DEPTH, DIFF_HD), 0.1),
        'diff_lambda_k2': nrm(ks[19], (DEPTH, DIFF_HD), 0.1),
        'diff_subln_w': gain(ks[20], (DEPTH, 2 * DIFF_HD)),
        'rel_bias': nrm(ks[21], (REL_BUCKETS, DIFF_HEADS), 0.3),
        'w_up_ret': nrm(ks[22], (DEPTH, RET_V_W, D_MODEL), RET_V_W ** -0.5),
        'w_up_diff': nrm(ks[23], (DEPTH, DIFF_V_W, D_MODEL), DIFF_V_W ** -0.5),
        'w_up_mem': nrm(ks[24], (DEPTH, MEM_W, D_MODEL), MEM_W ** -0.5),
        'w_out': nrm(ks[25], (DEPTH, D_MODEL, D_MODEL), D_MODEL ** -0.5),
        'peer_w_q': nrm(ks[26], (DEPTH, D_MODEL, PEER_HEADS * 2 * PEER_DK), D_MODEL ** -0.5),
        'peer_keys': nrm(ks[27], (DEPTH, PEER_HEADS, 2, PEER_NKEYS, PEER_DK), PEER_DK ** -0.5),
        'peer_u': nrm(ks[28], (DEPTH, PEER_EXPERTS, D_MODEL), D_MODEL ** -0.5),
        'peer_v': nrm(ks[29], (DEPTH, PEER_EXPERTS, D_MODEL), 0.25),
    }


def reference(x_prompt, x_sample, mem_prompt, state_ret, cache_diff_k, cache_diff_v, cache_mem_k, cache_mem_v,
              page_table, norm_mix_w, norm_mem_w, norm_ffn_w, norm_final_w, w_in, w_mem_k, w_mem_v,
              diff_lambda_q1, diff_lambda_k1, diff_lambda_q2, diff_lambda_k2, diff_subln_w, rel_bias,
              w_up_ret, w_up_diff, w_up_mem, w_out, peer_w_q, peer_keys, peer_u, peer_v):
    b, s, _ = x_prompt.shape
    db, ds, _ = x_sample.shape
    n_pages = page_table.shape[1]
    past = n_pages * cache_diff_k.shape[2]
    n_mem = mem_prompt.shape[1]
    pos_p = jnp.arange(s, dtype=jnp.int32)
    pos_s = past + jnp.arange(ds, dtype=jnp.int32)
    k_pos_s = jnp.arange(past + ds, dtype=jnp.int32)

    h_p, h_s = x_prompt, x_sample
    rs_p, dk_p, dv_p, mk_ps, mv_ps, rs_s, dk_s, dv_s = [], [], [], [], [], [], [], []
    for l in range(DEPTH):
        lam_init = lambda_init(l)
        lam = (jnp.exp(jnp.sum(diff_lambda_q1[l].astype(jnp.float32) * diff_lambda_k1[l].astype(jnp.float32)))
               - jnp.exp(jnp.sum(diff_lambda_q2[l].astype(jnp.float32) * diff_lambda_k2[l].astype(jnp.float32)))
               + lam_init)
        subln = diff_subln_w[l]
        lp = {'norm_mix_w': norm_mix_w[l], 'norm_ffn_w': norm_ffn_w[l], 'w_in': w_in[l],
              'w_up_ret': w_up_ret[l], 'w_up_diff': w_up_diff[l], 'w_up_mem': w_up_mem[l], 'w_out': w_out[l],
              'peer_w_q': peer_w_q[l], 'peer_keys': peer_keys[l], 'peer_u': peer_u[l], 'peer_v': peer_v[l]}

        mn = rmsnorm(mem_prompt, norm_mem_w[l])
        mk_p = (mn @ w_mem_k[l]).reshape(b, n_mem, MEM_HEADS, MEM_HD)
        mv_p = (mn @ w_mem_v[l]).reshape(b, n_mem, MEM_HEADS, MEM_HD)

        def diff_prompt(q, k, v, lam=lam, lam_init=lam_init, subln=subln):
            nb = s // Q_BLOCK
            qb = q.reshape(b, nb, Q_BLOCK, 2 * DIFF_HEADS, DIFF_HD).transpose(1, 0, 2, 3, 4)

            def blk(args):
                qi, i = args
                q_pos = i * Q_BLOCK + jnp.arange(Q_BLOCK, dtype=jnp.int32)
                return diff_attention(qi, k, v, q_pos, pos_p, lam, lam_init, rel_bias, subln)

            o = lax.map(blk, (qb, jnp.arange(nb, dtype=jnp.int32)))
            return o.transpose(1, 0, 2, 3).reshape(b, s, DIFF_V_W)

        def diff_sample(q, k, v, l=l, lam=lam, lam_init=lam_init, subln=subln):
            pk = cache_diff_k[l][page_table].reshape(db, past, 2 * DIFF_HEADS, DIFF_HD)
            pv = cache_diff_v[l][page_table].reshape(db, past, DIFF_HEADS, 2 * DIFF_HD)
            k_all = jnp.concatenate([pk.astype(k.dtype), k], axis=1)
            v_all = jnp.concatenate([pv.astype(v.dtype), v], axis=1)
            return diff_attention(q, k_all, v_all, pos_s, k_pos_s, lam, lam_init, rel_bias, subln)

        ret0_p = jnp.zeros((b, RET_HEADS, RET_DK, RET_DV), jnp.float32)
        h_p, st_p, k_p, v_p = decoder_layer(h_p, pos_p, ret0_p, mk_p, mv_p, diff_prompt, lp)
        h_s, st_s, k_s, v_s = decoder_layer(h_s, pos_s, state_ret[l], cache_mem_k[l], cache_mem_v[l], diff_sample, lp)
        rs_p.append(st_p); dk_p.append(k_p); dv_p.append(v_p); mk_ps.append(mk_p); mv_ps.append(mv_p)
        rs_s.append(st_s); dk_s.append(k_s); dv_s.append(v_s)

    y_prompt = rmsnorm(h_p, norm_final_w)
    y_sample = rmsnorm(h_s, norm_final_w)
    ret_state_prompt = jnp.stack(rs_p)
    diff_k_prompt = jnp.stack(dk_p)
    diff_v_prompt = jnp.stack(dv_p)
    mem_k_prompt = jnp.stack(mk_ps)
    mem_v_prompt = jnp.stack(mv_ps)
    ret_state_sample = jnp.stack(rs_s)
    diff_k_sample = jnp.stack(dk_s)
    diff_v_sample = jnp.stack(dv_s)
    return (y_prompt, y_sample, ret_state_prompt, diff_k_prompt, diff_v_prompt, mem_k_prompt, mem_v_prompt,
            ret_state_sample, diff_k_sample, diff_v_sample)
```

```python
import functools
import math

import numpy as np
import jax
import jax.numpy as jnp
from jax import lax
from jax.experimental import pallas as pl
from jax.experimental.pallas import tpu as pltpu

f32 = jnp.float32
bf16 = jnp.bfloat16

D_MODEL = 2048
RET_HEADS = 8
RET_DK = 128
RET_DV = 256
RET_CHUNK = 128
ROPE_BASE = 10000.0
DIFF_HEADS = 8
DIFF_HD = 64
REL_BUCKETS = 32
REL_MAX_DIST = 128
MEM_HEADS = 4
MEM_HD = 128
PEER_HEADS = 8
PEER_NKEYS = 128
PEER_TOPK = 16
PEER_DK = 128
EPS = 1e-6
NEG = -1e30

RET_QK_W = RET_HEADS * RET_DK
RET_V_W = RET_HEADS * RET_DV
DIFF_QK_W = 2 * DIFF_HEADS * DIFF_HD
DIFF_V_W = DIFF_HEADS * 2 * DIFF_HD
MEM_W = MEM_HEADS * MEM_HD
OFF_RQ = 0
OFF_RK = OFF_RQ + RET_QK_W
OFF_RV = OFF_RK + RET_QK_W
OFF_RG = OFF_RV + RET_V_W
OFF_DQ = OFF_RG + RET_V_W
OFF_DK = OFF_DQ + DIFF_QK_W
OFF_DV = OFF_DK + DIFF_QK_W
OFF_MQ = OFF_DV + DIFF_V_W
OFF_GA = OFF_MQ + MEM_W
OFF_GB = OFF_GA + D_MODEL
OFF_GC = OFF_GB + D_MODEL
IN_WIDTH = OFF_GC + D_MODEL

VMEM_LIMIT_BYTES = 56 * 1024 * 1024

NT = (((1,), (1,)), ((), ()))
TN = (((0,), (0,)), ((), ()))


def _cp(*sem):
    return pltpu.CompilerParams(dimension_semantics=sem, vmem_limit_bytes=VMEM_LIMIT_BYTES)


def _dot(a, b):
    return jnp.dot(a, b, preferred_element_type=f32)


def _dot_nt(a, b):
    return lax.dot_general(a, b, NT, preferred_element_type=f32)


def _dot_tn(a, b):
    return lax.dot_general(a, b, TN, preferred_element_type=f32)


def _rms_rows(x, g):
    y = x * lax.rsqrt(jnp.mean(x * x, axis=-1, keepdims=True) + EPS)
    return y * g


def _norm_matmul_kernel(x_ref, g_ref, w_ref, o_ref, xn_out_ref, xn_ref, *, rows):
    @pl.when(pl.program_id(1) == 0)
    def _():
        def body(r, c):
            sl = pl.ds(pl.multiple_of(r * rows, rows), rows)
            xn = _rms_rows(x_ref[sl, :], g_ref[...]).astype(bf16)
            xn_ref[sl, :] = xn
            xn_out_ref[sl, :] = xn
            return c
        lax.fori_loop(0, x_ref.shape[0] // rows, body, 0)

    o_ref[...] = _dot(xn_ref[...], w_ref[...])


def norm_matmul(x, g, w, *, tm=512, tn=512):
    t, d = x.shape
    n = w.shape[1]
    tm = min(tm, t)
    assert t % tm == 0 and n % tn == 0 and tm % 16 == 0
    rows = 32 if tm % 32 == 0 else 16
    return pl.pallas_call(
        functools.partial(_norm_matmul_kernel, rows=rows),
        grid=(t // tm, n // tn),
        in_specs=[
            pl.BlockSpec((tm, d), lambda i, j: (i, 0)),
            pl.BlockSpec((1, d), lambda i, j: (0, 0)),
            pl.BlockSpec((d, tn), lambda i, j: (0, j)),
        ],
        out_specs=[
            pl.BlockSpec((tm, tn), lambda i, j: (i, j)),
            pl.BlockSpec((tm, d), lambda i, j: (i, 0)),
        ],
        out_shape=[jax.ShapeDtypeStruct((t, n), f32), jax.ShapeDtypeStruct((t, d), bf16)],
        scratch_shapes=[pltpu.VMEM((tm, d), bf16)],
        compiler_params=_cp("parallel", "arbitrary"),
        name="norm_matmul",
    )(x, g.reshape(1, d), w)


def _ret_log_gamma():
    return jnp.log1p(-jnp.exp2(-5.0 - jnp.arange(RET_HEADS, dtype=f32)))


def _rope_tables(pos):
    half = RET_DK // 2
    inv = 1.0 / (ROPE_BASE ** jnp.linspace(0.0, 1.0, half, dtype=f32))
    ang = pos.astype(f32)[:, None] * inv[None, :]
    c, s = jnp.cos(ang), jnp.sin(ang)
    return jnp.concatenate([c, c], axis=-1), jnp.concatenate([-s, s], axis=-1)


def _rope(x, cos2, sin2):
    return x * cos2 + pltpu.roll(x, RET_DK // 2, 1) * sin2


def _ret_out(o, g):
    on = o * lax.rsqrt(jnp.mean(o * o, axis=-1, keepdims=True) + EPS)
    return (on * (g * jax.nn.sigmoid(g))).astype(bf16)


def _ret_prompt_kernel(decay_ref, q_ref, k_ref, v_ref, g_ref, cos_ref, sin_ref, dmask_ref, cs_ref, kdec_ref,
                       o_ref, st_out_ref, st_ref):
    c = pl.program_id(1)

    @pl.when(c == 0)
    def _():
        st_ref[...] = jnp.zeros_like(st_ref)

    cos2 = cos_ref[...]
    sin2 = sin_ref[...]
    for h in range(RET_HEADS):
        ks = slice(h * RET_DK, (h + 1) * RET_DK)
        vs = slice(h * RET_DV, (h + 1) * RET_DV)
        q = _rope(q_ref[:, ks], cos2, sin2)
        k = _rope(k_ref[:, ks], cos2, sin2) * (RET_DK ** -0.5)
        v = v_ref[:, vs]
        st = st_ref[h]
        qk = _dot_nt(q, k) * dmask_ref[h]
        o = _dot(qk, v) + _dot(q, st) * cs_ref[h]
        st_ref[h] = decay_ref[h] * st + _dot_tn(k * kdec_ref[h], v)
        o_ref[:, vs] = _ret_out(o, g_ref[:, vs])

    @pl.when(c == pl.num_programs(1) - 1)
    def _():
        st_out_ref[...] = st_ref[...]


def retention_prompt(z, batch, seq):
    cl = RET_CHUNK
    assert seq % cl == 0
    nc = seq // cl
    lg = _ret_log_gamma()[:, None]
    idx = jnp.arange(cl, dtype=f32)
    rel = idx[:, None] - idx[None, :]
    dmask = jnp.where(rel >= 0, jnp.exp(lg[:, :, None] * jnp.maximum(rel, 0.0)), 0.0)
    cs = jnp.broadcast_to(jnp.exp(lg * (idx + 1.0))[:, :, None], (RET_HEADS, cl, RET_DV))
    kdec = jnp.broadcast_to(jnp.exp(lg * (cl - 1.0 - idx))[:, :, None], (RET_HEADS, cl, RET_DK))
    decay = jnp.exp(lg[:, 0] * cl)
    cos2, sin2 = _rope_tables(jnp.arange(seq, dtype=jnp.int32))
    row = lambda b, c: b * nc + c
    const3 = lambda b, c: (0, 0, 0)
    return pl.pallas_call(
        _ret_prompt_kernel,
        grid=(batch, nc),
        in_specs=[
            pl.BlockSpec(memory_space=pltpu.SMEM),
            pl.BlockSpec((cl, RET_QK_W), lambda b, c: (row(b, c), OFF_RQ // RET_QK_W)),
            pl.BlockSpec((cl, RET_QK_W), lambda b, c: (row(b, c), OFF_RK // RET_QK_W)),
            pl.BlockSpec((cl, RET_V_W), lambda b, c: (row(b, c), OFF_RV // RET_V_W)),
            pl.BlockSpec((cl, RET_V_W), lambda b, c: (row(b, c), OFF_RG // RET_V_W)),
            pl.BlockSpec((cl, RET_DK), lambda b, c: (c, 0)),
            pl.BlockSpec((cl, RET_DK), lambda b, c: (c, 0)),
            pl.BlockSpec((RET_HEADS, cl, cl), const3),
            pl.BlockSpec((RET_HEADS, cl, RET_DV), const3),
            pl.BlockSpec((RET_HEADS, cl, RET_DK), const3),
        ],
        out_specs=[
            pl.BlockSpec((cl, RET_V_W), lambda b, c: (row(b, c), 0)),
            pl.BlockSpec((None, RET_HEADS, RET_DK, RET_DV), lambda b, c: (b, 0, 0, 0)),
        ],
        out_shape=[
            jax.ShapeDtypeStruct((batch * seq, RET_V_W), bf16),
            jax.ShapeDtypeStruct((batch, RET_HEADS, RET_DK, RET_DV), f32),
        ],
        scratch_shapes=[pltpu.VMEM((RET_HEADS, RET_DK, RET_DV), f32)],
        compiler_params=_cp("parallel", "arbitrary"),
        name="retention_prompt",
    )(decay, z, z, z, z, cos2, sin2, dmask, cs, kdec)


RET_SAMPLE_GROUP = 4


def _ret_sample_kernel(decay_ref, q_ref, k_ref, v_ref, g_ref, st_in_ref, cos_ref, sin_ref, dmask_ref, cs_ref,
                       kdec_ref, o_ref, st_out_ref, *, ds):
    cos2 = cos_ref[...]
    sin2 = sin_ref[...]
    rows = q_ref.shape[0]
    seq_of_row = lax.broadcasted_iota(jnp.int32, (rows, RET_DV), 0) // ds
    for h in range(RET_HEADS):
        ks = slice(h * RET_DK, (h + 1) * RET_DK)
        vs = slice(h * RET_DV, (h + 1) * RET_DV)
        q = _rope(q_ref[:, ks], cos2, sin2)
        k = _rope(k_ref[:, ks], cos2, sin2) * (RET_DK ** -0.5)
        v = v_ref[:, vs]
        kd = k * kdec_ref[h]
        qk = _dot_nt(q, k) * dmask_ref[h]
        cross = jnp.zeros((rows, RET_DV), f32)
        for s in range(RET_SAMPLE_GROUP):
            st = st_in_ref[s, h]
            cross = jnp.where(seq_of_row == s, _dot(q, st), cross)
            vb = jnp.where(seq_of_row == s, v, 0.0)
            st_out_ref[s, h] = decay_ref[h] * st + _dot_tn(kd, vb)
        o = _dot(qk, v) + cross * cs_ref[h]
        o_ref[:, vs] = _ret_out(o, g_ref[:, vs])


def retention_sample(z, state0, n_seq, ds, past):
    g = RET_SAMPLE_GROUP
    assert n_seq % g == 0 and (g * ds) % 8 == 0 and (ds % RET_CHUNK != 0)
    rows = g * ds
    lg = _ret_log_gamma()[:, None]
    idx = jnp.arange(ds, dtype=f32)
    rel = idx[:, None] - idx[None, :]
    dm = jnp.where(rel >= 0, jnp.exp(lg[:, :, None] * jnp.maximum(rel, 0.0)), 0.0)
    same_seq = jnp.kron(jnp.eye(g, dtype=f32), jnp.ones((ds, ds), f32))
    dmask = jnp.tile(dm, (1, g, g)) * same_seq[None]
    cs = jnp.broadcast_to(jnp.tile(jnp.exp(lg * (idx + 1.0)), (1, g))[:, :, None], (RET_HEADS, rows, RET_DV))
    kdec = jnp.broadcast_to(jnp.tile(jnp.exp(lg * (ds - 1.0 - idx)), (1, g))[:, :, None], (RET_HEADS, rows, RET_DK))
    decay = jnp.exp(lg[:, 0] * ds)
    cos2, sin2 = _rope_tables(past + jnp.arange(ds, dtype=jnp.int32))
    cos2, sin2 = jnp.tile(cos2, (g, 1)), jnp.tile(sin2, (g, 1))
    const2 = lambda i: (0, 0)
    const3 = lambda i: (0, 0, 0)
    st_spec = pl.BlockSpec((g, RET_HEADS, RET_DK, RET_DV), lambda i: (i, 0, 0, 0))
    return pl.pallas_call(
        functools.partial(_ret_sample_kernel, ds=ds),
        grid=(n_seq // g,),
        in_specs=[
            pl.BlockSpec(memory_space=pltpu.SMEM),
            pl.BlockSpec((rows, RET_QK_W), lambda i: (i, OFF_RQ // RET_QK_W)),
            pl.BlockSpec((rows, RET_QK_W), lambda i: (i, OFF_RK // RET_QK_W)),
            pl.BlockSpec((rows, RET_V_W), lambda i: (i, OFF_RV // RET_V_W)),
            pl.BlockSpec((rows, RET_V_W), lambda i: (i, OFF_RG // RET_V_W)),
            st_spec,
            pl.BlockSpec((rows, RET_DK), const2),
            pl.BlockSpec((rows, RET_DK), const2),
            pl.BlockSpec((RET_HEADS, rows, rows), const3),
            pl.BlockSpec((RET_HEADS, rows, RET_DV), const3),
            pl.BlockSpec((RET_HEADS, rows, RET_DK), const3),
        ],
        out_specs=[pl.BlockSpec((rows, RET_V_W), lambda i: (i, 0)), st_spec],
        out_shape=[
            jax.ShapeDtypeStruct((n_seq * ds, RET_V_W), bf16),
            jax.ShapeDtypeStruct((n_seq, RET_HEADS, RET_DK, RET_DV), f32),
        ],
        compiler_params=_cp("parallel"),
        name="retention_sample",
    )(decay, z, z, z, z, state0, cos2, sin2, dmask, cs, kdec)


def _lambda_init(layer):
    return 0.8 - 0.6 * math.exp(-0.3 * layer)


def _lambda_kernel(q1_ref, k1_ref, q2_ref, k2_ref, o_ref, *, lam_init):
    a = jnp.sum(q1_ref[...] * k1_ref[...], axis=-1, keepdims=True)
    b = jnp.sum(q2_ref[...] * k2_ref[...], axis=-1, keepdims=True)
    o_ref[...] = jnp.broadcast_to(jnp.exp(a) - jnp.exp(b) + lam_init, o_ref.shape)


def diff_lambda(q1, k1, q2, k2, lam_init):
    r = lambda a: a.reshape(1, DIFF_HD)
    return pl.pallas_call(
        functools.partial(_lambda_kernel, lam_init=lam_init),
        out_shape=jax.ShapeDtypeStruct((1, 2 * DIFF_HD), f32),
        name="diff_lambda",
    )(r(q1), r(k1), r(q2), r(k2))


def _rel_bucket_np(rel):
    n = np.maximum(rel, 0)
    max_exact = REL_BUCKETS // 2
    nf = np.maximum(n, 1).astype(np.float32)
    large = max_exact + (np.log(nf / np.float32(max_exact)) / np.float32(math.log(REL_MAX_DIST / max_exact))
                         * np.float32(REL_BUCKETS - max_exact)).astype(np.int32)
    large = np.minimum(large, REL_BUCKETS - 1)
    return np.where(n < max_exact, n, large).astype(np.int32)


def _bias_expand_kernel(rb_ref, tbl_ref, o_ref):
    h = pl.program_id(0)
    tbl = tbl_ref[...]
    acc = jnp.full(tbl.shape, NEG, f32)
    for bkt in range(REL_BUCKETS):
        acc = jnp.where(tbl == bkt, rb_ref[bkt, h], acc)
    o_ref[...] = acc


def bias_expand(bucket_tbl, rel_bias):
    r, c = bucket_tbl.shape
    return pl.pallas_call(
        _bias_expand_kernel,
        grid=(DIFF_HEADS,),
        in_specs=[pl.BlockSpec(memory_space=pltpu.SMEM), pl.BlockSpec((r, c), lambda h: (0, 0))],
        out_specs=pl.BlockSpec((None, r, c), lambda h: (h, 0, 0)),
        out_shape=jax.ShapeDtypeStruct((DIFF_HEADS, r, c), f32),
        compiler_params=_cp("parallel"),
        name="bias_expand",
    )(rel_bias.astype(f32), bucket_tbl)


def _subln_out(o, w, lam_init):
    on = o * lax.rsqrt(jnp.mean(o * o, axis=-1, keepdims=True) + EPS)
    return (on * w * (1.0 - lam_init)).astype(bf16)


def _softmax_step(s, v, m_old, l_old, acc_old):
    m_new = jnp.maximum(m_old, jnp.max(s, axis=-1, keepdims=True))
    p = jnp.exp(s - m_new)
    alpha = jnp.exp(m_old - m_new)
    return m_new, alpha * l_old + jnp.sum(p, axis=-1, keepdims=True), alpha * acc_old + _dot(p, v)


DIFF_TILE = 256


def _diff_prompt_kernel(lam_ref, q_ref, k_ref, v_ref, bias_ref, w_ref, o_ref, m_ref, l_ref, acc_ref, *, lam_init):
    qi = pl.program_id(2)
    t = DIFF_TILE
    q = q_ref[...] * (DIFF_HD ** -0.5)
    lane = lax.broadcasted_iota(jnp.int32, q.shape, 1)
    qm = (jnp.where(lane < DIFF_HD, q, 0.0), jnp.where(lane >= DIFF_HD, q, 0.0))
    m_ref[...] = jnp.full(m_ref.shape, NEG, f32)
    l_ref[...] = jnp.zeros_like(l_ref)
    acc_ref[...] = jnp.zeros_like(acc_ref)

    def body(kj, carry):
        sl = pl.ds(pl.multiple_of(kj * t, t), t)
        kb = k_ref[sl, :]
        vb = v_ref[sl, :]
        bias = bias_ref[jnp.minimum(qi - kj, 2)]
        for mp in range(2):
            s = _dot_nt(qm[mp], kb) + bias
            m_ref[mp], l_ref[mp], acc_ref[mp] = _softmax_step(s, vb, m_ref[mp], l_ref[mp], acc_ref[mp])
        return carry

    lax.fori_loop(0, qi + 1, body, 0)
    o = acc_ref[0] / l_ref[0] - lam_ref[...] * (acc_ref[1] / l_ref[1])
    o_ref[...] = _subln_out(o, w_ref[...], lam_init)


def diff_attn_prompt(z, lam, rel_bias, subln_w, batch, seq, lam_init):
    t = DIFF_TILE
    assert seq % t == 0
    nq = seq // t
    i = np.arange(t)
    rel0 = i[:, None] - i[None, :]
    tb0 = np.where(rel0 >= 0, _rel_bucket_np(rel0), -1)
    tb1 = _rel_bucket_np(rel0 + t)
    far = _rel_bucket_np(np.arange(t + 1, max(seq, 2 * t + 1)))
    assert (far == far[0]).all()
    tb2 = np.full((t, t), far[0], np.int32)
    tbl = jnp.asarray(np.concatenate([tb0, tb1, tb2], axis=0).astype(np.int32))
    bias = bias_expand(tbl, rel_bias).reshape(DIFF_HEADS, 3, t, t)
    w2 = 2 * DIFF_HD
    return pl.pallas_call(
        functools.partial(_diff_prompt_kernel, lam_init=lam_init),
        grid=(batch, DIFF_HEADS, nq),
        in_specs=[
            pl.BlockSpec((1, w2), lambda b, h, i: (0, 0)),
            pl.BlockSpec((t, w2), lambda b, h, i: (b * nq + i, OFF_DQ // w2 + h)),
            pl.BlockSpec((seq, w2), lambda b, h, i: (b, OFF_DK // w2 + h)),
            pl.BlockSpec((seq, w2), lambda b, h, i: (b, OFF_DV // w2 + h)),
            pl.BlockSpec((None, 3, t, t), lambda b, h, i: (h, 0, 0, 0)),
            pl.BlockSpec((1, w2), lambda b, h, i: (0, 0)),
        ],
        out_specs=pl.BlockSpec((t, w2), lambda b, h, i: (b * nq + i, h)),
        out_shape=jax.ShapeDtypeStruct((batch * seq, DIFF_V_W), bf16),
        scratch_shapes=[pltpu.VMEM((2, t, 1), f32), pltpu.VMEM((2, t, 1), f32), pltpu.VMEM((2, t, w2), f32)],
        compiler_params=_cp("parallel", "parallel", "arbitrary"),
        name="diff_attn_prompt",
    )(lam, z, z, z, bias, subln_w.reshape(1, w2).astype(f32))


def _diff_sample_kernel(pt_ref, lam_ref, q_ref, kn_ref, vn_ref, kp_ref, vp_ref, bias_ref, w_ref, o_ref,
                        qbd_ref, kpad_ref, vpad_ref, m_ref, l_ref, acc_ref, *, ds, n_pages, lam_init):
    b = pl.program_id(0)
    p = pl.program_id(1)
    n_maps = 2 * DIFF_HEADS
    rows = n_maps * ds

    @pl.when((b == 0) & (p == 0))
    def _():
        kpad_ref[...] = jnp.zeros_like(kpad_ref)
        vpad_ref[...] = jnp.zeros_like(vpad_ref)

    @pl.when(p == 0)
    def _():
        q = q_ref[...] * (DIFF_HD ** -0.5)
        qrep = jnp.concatenate([q] * n_maps, axis=0)
        row_map = lax.broadcasted_iota(jnp.int32, (rows, DIFF_QK_W), 0) // ds
        col_map = lax.broadcasted_iota(jnp.int32, (rows, DIFF_QK_W), 1) // DIFF_HD
        qbd_ref[...] = jnp.where(row_map == col_map, qrep, 0.0)
        kpad_ref[0:ds, :] = kn_ref[...]
        vpad_ref[0:ds, :] = vn_ref[...]
        m_ref[...] = jnp.full(m_ref.shape, NEG, f32)
        l_ref[...] = jnp.zeros_like(l_ref)
        acc_ref[...] = jnp.zeros_like(acc_ref)

    def process(kb, vb, bias):
        s = _dot_nt(qbd_ref[...], kb) + bias
        m_ref[...], l_ref[...], acc_ref[...] = _softmax_step(s, vb, m_ref[...], l_ref[...], acc_ref[...])

    process(kp_ref[...], vp_ref[...], bias_ref[p])

    @pl.when(p == n_pages - 1)
    def _():
        process(kpad_ref[...], vpad_ref[...], bias_ref[n_pages])
        w2 = 2 * DIFF_HD
        for h in range(DIFF_HEADS):
            rs = slice(2 * ds * h, 2 * ds * (h + 1))
            blk = acc_ref[rs, h * w2:(h + 1) * w2] / l_ref[rs, :]
            o = blk[0:ds] - lam_ref[...] * blk[ds:2 * ds]
            o_ref[:, h * w2:(h + 1) * w2] = _subln_out(o, w_ref[...], lam_init)


def diff_attn_sample(z3, cache_k, cache_v, page_table, lam, rel_bias, subln_w, lam_init):
    n_seq, ds, _ = z3.shape
    n_pages = page_table.shape[1]
    page = cache_k.shape[1]
    past = n_pages * page
    n_maps = 2 * DIFF_HEADS
    rows = n_maps * ds
    assert page == 128 and 2 * ds == 8
    qi = np.arange(ds)
    tb = np.full((n_pages + 1, 2, ds, page), -1, np.int32)
    for pg in range(n_pages):
        rel = past + qi[:, None] - (pg * page + np.arange(page)[None, :])
        tb[pg] = _rel_bucket_np(rel)[None]
    rel_new = qi[:, None] - qi[None, :]
    tb[n_pages, :, :, :ds] = np.where(rel_new >= 0, _rel_bucket_np(rel_new), -1)[None]
    bias = bias_expand(jnp.asarray(tb.reshape((n_pages + 1) * 2 * ds, page)), rel_bias)
    bias = bias.reshape(DIFF_HEADS, n_pages + 1, 2 * ds, page).transpose(1, 0, 2, 3).reshape(n_pages + 1, rows, page)
    w2 = 2 * DIFF_HD
    grid_spec = pltpu.PrefetchScalarGridSpec(
        num_scalar_prefetch=1,
        grid=(n_seq, n_pages),
        in_specs=[
            pl.BlockSpec((1, w2), lambda b, p, pt: (0, 0)),
            pl.BlockSpec((None, ds, DIFF_QK_W), lambda b, p, pt: (b, 0, OFF_DQ // DIFF_QK_W)),
            pl.BlockSpec((None, ds, DIFF_QK_W), lambda b, p, pt: (b, 0, OFF_DK // DIFF_QK_W)),
            pl.BlockSpec((None, ds, DIFF_V_W), lambda b, p, pt: (b, 0, OFF_DV // DIFF_V_W)),
            pl.BlockSpec((None, page, DIFF_QK_W), lambda b, p, pt: (pt[b * n_pages + p], 0, 0)),
            pl.BlockSpec((None, page, DIFF_V_W), lambda b, p, pt: (pt[b * n_pages + p], 0, 0)),
            pl.BlockSpec((n_pages + 1, rows, page), lambda b, p, pt: (0, 0, 0)),
            pl.BlockSpec((1, w2), lambda b, p, pt: (0, 0)),
        ],
        out_specs=pl.BlockSpec((None, ds, DIFF_V_W), lambda b, p, pt: (b, 0, 0)),
        scratch_shapes=[
            pltpu.VMEM((rows, DIFF_QK_W), f32),
            pltpu.VMEM((page, DIFF_QK_W), f32),
            pltpu.VMEM((page, DIFF_V_W), f32),
            pltpu.VMEM((rows, 1), f32),
            pltpu.VMEM((rows, 1), f32),
            pltpu.VMEM((rows, DIFF_V_W), f32),
        ],
    )
    return pl.pallas_call(
        functools.partial(_diff_sample_kernel, ds=ds, n_pages=n_pages, lam_init=lam_init),
        grid_spec=grid_spec,
        out_shape=jax.ShapeDtypeStruct((n_seq, ds, DIFF_V_W), bf16),
        compiler_params=_cp("arbitrary", "arbitrary"),
        name="diff_attn_sample",
    )(page_table.reshape(-1), lam, z3, z3, z3, cache_k, cache_v, bias, subln_w.reshape(1, w2).astype(f32))


def _mem_attn_kernel(q_ref, mk_ref, mv_ref, o_ref):
    for h in range(MEM_HEADS):
        hs = slice(h * MEM_HD, (h + 1) * MEM_HD)
        s = _dot_nt(q_ref[:, hs], mk_ref[:, hs]) * (MEM_HD ** -0.5)
        m = jnp.max(s, axis=-1, keepdims=True)
        p = jnp.exp(s - m)
        o = _dot(p, mv_ref[:, hs]) / jnp.sum(p, axis=-1, keepdims=True)
        o_ref[:, hs] = o.astype(bf16)


def mem_attn_prompt(z, mkv, batch, seq, n_mem, *, tq=256):
    assert seq % tq == 0
    nq = seq // tq
    return pl.pallas_call(
        _mem_attn_kernel,
        grid=(batch, nq),
        in_specs=[
            pl.BlockSpec((tq, MEM_W), lambda b, i: (b * nq + i, OFF_MQ // MEM_W)),
            pl.BlockSpec((n_mem, MEM_W), lambda b, i: (b, 0)),
            pl.BlockSpec((n_mem, MEM_W), lambda b, i: (b, 1)),
        ],
        out_specs=pl.BlockSpec((tq, MEM_W), lambda b, i: (b * nq + i, 0)),
        out_shape=jax.ShapeDtypeStruct((batch * seq, MEM_W), bf16),
        compiler_params=_cp("parallel", "parallel"),
        name="mem_attn_prompt",
    )(z, mkv, mkv)


def mem_attn_sample(z3, mk, mv):
    n_seq, ds, _ = z3.shape
    n_mem = mk.shape[1]
    return pl.pallas_call(
        _mem_attn_kernel,
        grid=(n_seq,),
        in_specs=[
            pl.BlockSpec((None, ds, MEM_W), lambda b: (b, 0, OFF_MQ // MEM_W)),
            pl.BlockSpec((None, n_mem, MEM_W), lambda b: (b, 0, 0)),
            pl.BlockSpec((None, n_mem, MEM_W), lambda b: (b, 0, 0)),
        ],
        out_specs=pl.BlockSpec((None, ds, MEM_W), lambda b: (b, 0, 0)),
        out_shape=jax.ShapeDtypeStruct((n_seq, ds, MEM_W), bf16),
        compiler_params=_cp("parallel"),
        name="mem_attn_sample",
    )(z3, mk, mv)


def _mix_kernel(orr_ref, od_ref, om_ref, ga_ref, gb_ref, gc_ref, wr_ref, wd_ref, wm_ref, o_ref):
    mixed = (jax.nn.sigmoid(ga_ref[...]) * _dot(orr_ref[...], wr_ref[...])
             + jax.nn.sigmoid(gb_ref[...]) * _dot(od_ref[...], wd_ref[...])
             + jax.nn.sigmoid(gc_ref[...]) * _dot(om_ref[...], wm_ref[...]))
    o_ref[...] = mixed.astype(bf16)


def gated_mix(o_r, o_d, o_m, z, w_r, w_d, w_m, *, tm=512, tn=512):
    t = o_r.shape[0]
    tm = min(tm, t)
    assert t % tm == 0 and D_MODEL % tn == 0
    gate = lambda off: pl.BlockSpec((tm, tn), lambda i, j: (i, off // tn + j))
    act = lambda w: pl.BlockSpec((tm, w), lambda i, j: (i, 0))
    wgt = lambda w: pl.BlockSpec((w, tn), lambda i, j: (0, j))
    return pl.pallas_call(
        _mix_kernel,
        grid=(t // tm, D_MODEL // tn),
        in_specs=[act(RET_V_W), act(DIFF_V_W), act(MEM_W), gate(OFF_GA), gate(OFF_GB), gate(OFF_GC),
                  wgt(RET_V_W), wgt(DIFF_V_W), wgt(MEM_W)],
        out_specs=pl.BlockSpec((tm, tn), lambda i, j: (i, j)),
        out_shape=jax.ShapeDtypeStruct((t, D_MODEL), bf16),
        compiler_params=_cp("parallel", "parallel"),
        name="gated_mix",
    )(o_r, o_d, o_m, z, z, z, w_r, w_d, w_m)


def _matmul_residual_kernel(x_ref, a_ref, w_ref, o_ref):
    o_ref[...] = x_ref[...] + _dot(a_ref[...], w_ref[...])


def matmul_residual(x, a, w, *, tm=512, tn=512):
    t, k = a.shape
    n = w.shape[1]
    tm = min(tm, t)
    assert t % tm == 0 and n % tn == 0
    return pl.pallas_call(
        _matmul_residual_kernel,
        grid=(t // tm, n // tn),
        in_specs=[
            pl.BlockSpec((tm, tn), lambda i, j: (i, j)),
            pl.BlockSpec((tm, k), lambda i, j: (i, 0)),
            pl.BlockSpec((k, tn), lambda i, j: (0, j)),
        ],
        out_specs=pl.BlockSpec((tm, tn), lambda i, j: (i, j)),
        out_shape=jax.ShapeDtypeStruct((t, n), f32),
        compiler_params=_cp("parallel", "parallel"),
        name="matmul_residual",
    )(x, a, w)


_PEER_CAND = [(a, b) for a in range(PEER_TOPK) for b in range(PEER_TOPK) if (a + 1) * (b + 1) <= PEER_TOPK]


def _top_values(x, k):
    out = []
    for i in range(k):
        mx = jnp.max(x, axis=0, keepdims=True)
        out.append(mx)
        if i + 1 < k:
            x = jnp.where(x == mx, -jnp.inf, x)
    return out


def _peer_select_kernel(pq_ref, keys_ref, a1_ref, a2_ref, e1_ref, e2_ref, tau_ref):
    for h in range(PEER_HEADS):
        sc = []
        for c in range(2):
            off = (2 * h + c) * PEER_DK
            sc.append(_dot_nt(keys_ref[h, c], pq_ref[:, off:off + PEER_DK]))
        a1_ref[h] = sc[0]
        a2_ref[h] = sc[1]
        t1 = _top_values(sc[0], PEER_TOPK)
        t2 = _top_values(sc[1], PEER_TOPK)
        cand = jnp.concatenate([t1[a] + t2[b] for a, b in _PEER_CAND], axis=0)
        top = _top_values(cand, PEER_TOPK)
        m = top[0]
        zsum = jnp.ones_like(m)
        for r in range(1, PEER_TOPK):
            zsum = zsum + jnp.exp(top[r] - m)
        tau_ref[h:h + 1, :] = top[PEER_TOPK - 1]
        e1_ref[h] = jnp.exp(sc[0] - t1[0]) / zsum
        e2_ref[h] = jnp.exp(sc[1] - t2[0])


def peer_select(pq, keys, *, tt=256):
    t = pq.shape[0]
    tt = min(tt, t)
    assert t % tt == 0
    big = pl.BlockSpec((PEER_HEADS, PEER_NKEYS, tt), lambda i: (0, 0, i))
    shp = jax.ShapeDtypeStruct((PEER_HEADS, PEER_NKEYS, t), f32)
    return pl.pallas_call(
        _peer_select_kernel,
        grid=(t // tt,),
        in_specs=[
            pl.BlockSpec((tt, PEER_HEADS * 2 * PEER_DK), lambda i: (i, 0)),
            pl.BlockSpec((PEER_HEADS, 2, PEER_NKEYS, PEER_DK), lambda i: (0, 0, 0, 0)),
        ],
        out_specs=[big, big, big, big, pl.BlockSpec((PEER_HEADS, tt), lambda i: (0, i))],
        out_shape=[shp, shp, shp, shp, jax.ShapeDtypeStruct((PEER_HEADS, t), f32)],
        compiler_params=_cp("parallel"),
        name="peer_select",
    )(pq, keys)


PEER_ROWS = 8


def _gelu(x):
    return 0.5 * x * (1.0 + lax.erf(x * (2.0 ** -0.5)))


def _peer_dense_kernel(xn_ref, u_ref, vt_ref, a1_ref, e1_ref, a2_ref, e2_ref, tau_ref, o_ref,
                       ht_ref, wt_ref, acc_ref):
    j = pl.program_id(1)

    @pl.when(j == 0)
    def _():
        acc_ref[...] = jnp.zeros_like(acc_ref)

    ht_ref[...] = _dot_nt(u_ref[...], xn_ref[...])
    tt = ht_ref.shape[1]
    for r in range(PEER_ROWS):
        rs = slice(r * PEER_NKEYS, (r + 1) * PEER_NKEYS)
        for c0 in range(0, tt, 128):
            cs = slice(c0, c0 + 128)
            g = jnp.zeros((PEER_NKEYS, 128), f32)
            for h in range(PEER_HEADS):
                sc = a1_ref[h, r:r + 1, cs] + a2_ref[h, :, cs]
                w = e1_ref[h, r:r + 1, cs] * e2_ref[h, :, cs]
                g = g + jnp.where(sc >= tau_ref[h:h + 1, cs], w, 0.0)
            wt_ref[rs, cs] = (g * _gelu(ht_ref[rs, cs])).astype(bf16)
    acc_ref[...] += _dot(vt_ref[...], wt_ref[...])

    @pl.when(j == pl.num_programs(1) - 1)
    def _():
        o_ref[...] = acc_ref[...].T


def peer_dense(xn, u, vt, a1, a2, e1, e2, tau, *, tt=512):
    t, d = xn.shape
    n_exp = u.shape[0]
    tt = min(tt, t)
    ec = PEER_ROWS * PEER_NKEYS
    assert t % tt == 0 and n_exp % ec == 0 and tt % 128 == 0
    once = pl.Buffered(1)
    row_blk = pl.BlockSpec((PEER_HEADS, PEER_ROWS, tt), lambda i, j: (0, j, i))
    all_blk = pl.BlockSpec((PEER_HEADS, PEER_NKEYS, tt), lambda i, j: (0, 0, i), pipeline_mode=once)
    return pl.pallas_call(
        _peer_dense_kernel,
        grid=(t // tt, n_exp // ec),
        in_specs=[
            pl.BlockSpec((tt, d), lambda i, j: (i, 0), pipeline_mode=once),
            pl.BlockSpec((ec, d), lambda i, j: (j, 0)),
            pl.BlockSpec((d, ec), lambda i, j: (0, j)),
            row_blk, row_blk, all_blk, all_blk,
            pl.BlockSpec((PEER_HEADS, tt), lambda i, j: (0, i)),
        ],
        out_specs=pl.BlockSpec((tt, d), lambda i, j: (i, 0)),
        out_shape=jax.ShapeDtypeStruct((t, d), f32),
        scratch_shapes=[pltpu.VMEM((ec, tt), f32), pltpu.VMEM((ec, tt), bf16), pltpu.VMEM((d, tt), f32)],
        compiler_params=_cp("parallel", "arbitrary"),
        name="peer_dense",
    )(xn, u, vt, a1, e1, a2, e2, tau)


def _final_norm_kernel(h_ref, p_ref, g_ref, o_ref):
    o_ref[...] = _rms_rows(h_ref[...] + p_ref[...], g_ref[...])


def final_norm(h, p, g, *, tm=256):
    t, d = h.shape
    tm = min(tm, t)
    assert t % tm == 0
    blk = pl.BlockSpec((tm, d), lambda i: (i, 0))
    return pl.pallas_call(
        _final_norm_kernel,
        grid=(t // tm,),
        in_specs=[blk, blk, pl.BlockSpec((1, d), lambda i: (0, 0))],
        out_specs=blk,
        out_shape=jax.ShapeDtypeStruct((t, d), f32),
        compiler_params=_cp("parallel"),
        name="final_norm",
    )(h, p, g.reshape(1, d))


def _token_tail(x2, z, o_r, o_d, o_m, wts):
    mixed = gated_mix(o_r, o_d, o_m, z, wts["w_up_ret"], wts["w_up_diff"], wts["w_up_mem"])
    h = matmul_residual(x2, mixed, wts["w_out"])
    pq, xn = norm_matmul(h, wts["norm_ffn_w"], wts["peer_w_q"])
    a1, a2, e1, e2, tau = peer_select(pq, wts["peer_keys"])
    peer = peer_dense(xn, wts["peer_u"], wts["peer_vt"], a1, a2, e1, e2, tau)
    return final_norm(h, peer, wts["norm_final_w"])


def kernel(x_prompt, x_sample, mem_prompt, state_ret, cache_diff_k, cache_diff_v, cache_mem_k, cache_mem_v, page_table, norm_mix_w, norm_mem_w, norm_ffn_w, norm_final_w, w_in, w_mem_k, w_mem_v, diff_lambda_q1, diff_lambda_k1, diff_lambda_q2, diff_lambda_k2, diff_subln_w, rel_bias, w_up_ret, w_up_diff, w_up_mem, w_out, peer_w_q, peer_keys, peer_u, peer_v):
    b, s, d = x_prompt.shape
    db, ds, _ = x_sample.shape
    n_pages = page_table.shape[1]
    page = cache_diff_k.shape[2]
    past = n_pages * page
    n_mem = mem_prompt.shape[1]
    assert w_in.shape[0] == 1 and d == D_MODEL, "single-layer trunk"
    l = 0
    lam_init = _lambda_init(l)

    wts = {
        "w_up_ret": w_up_ret[l].astype(bf16), "w_up_diff": w_up_diff[l].astype(bf16),
        "w_up_mem": w_up_mem[l].astype(bf16), "w_out": w_out[l].astype(bf16),
        "norm_ffn_w": norm_ffn_w[l], "peer_w_q": peer_w_q[l].astype(bf16), "peer_keys": peer_keys[l],
        "peer_u": peer_u[l].astype(bf16), "peer_vt": peer_v[l].astype(bf16).T, "norm_final_w": norm_final_w,
    }
    w_in_b = w_in[l].astype(bf16)
    lam = diff_lambda(diff_lambda_q1[l], diff_lambda_k1[l], diff_lambda_q2[l], diff_lambda_k2[l], lam_init)

    xp = x_prompt.reshape(b * s, d)
    zp, _ = norm_matmul(xp, norm_mix_w[l], w_in_b)
    w_mem = jnp.concatenate([w_mem_k[l], w_mem_v[l]], axis=1).astype(bf16)
    mkv, _ = norm_matmul(mem_prompt.reshape(b * n_mem, d), norm_mem_w[l], w_mem)
    o_r, st_p = retention_prompt(zp, b, s)
    o_d = diff_attn_prompt(zp, lam, rel_bias, diff_subln_w[l], b, s, lam_init)
    o_m = mem_attn_prompt(zp, mkv, b, s, n_mem)
    y_prompt = _token_tail(xp, zp, o_r, o_d, o_m, wts).reshape(b, s, d)

    xs = x_sample.reshape(db * ds, d)
    zs, _ = norm_matmul(xs, norm_mix_w[l], w_in_b)
    zs3 = zs.reshape(db, ds, IN_WIDTH)
    o_r_s, st_s = retention_sample(zs, state_ret[l], db, ds, past)
    ck = cache_diff_k[l].reshape(-1, page, DIFF_QK_W)
    cv = cache_diff_v[l].reshape(-1, page, DIFF_V_W)
    o_d_s = diff_attn_sample(zs3, ck, cv, page_table, lam, rel_bias, diff_subln_w[l], lam_init)
    o_m_s = mem_attn_sample(zs3, cache_mem_k[l].reshape(db, n_mem, MEM_W), cache_mem_v[l].reshape(db, n_mem, MEM_W))
    y_sample = _token_tail(xs, zs, o_r_s, o_d_s.reshape(db * ds, DIFF_V_W), o_m_s.reshape(db * ds, MEM_W),
                           wts).reshape(db, ds, d)

    nh2 = 2 * DIFF_HEADS
    return (
        y_prompt,
        y_sample,
        st_p[None],
        zp[:, OFF_DK:OFF_DK + DIFF_QK_W].reshape(1, b, s, nh2, DIFF_HD),
        zp[:, OFF_DV:OFF_DV + DIFF_V_W].reshape(1, b, s, DIFF_HEADS, 2 * DIFF_HD),
        mkv[:, :MEM_W].reshape(1, b, n_mem, MEM_HEADS, MEM_HD),
        mkv[:, MEM_W:].reshape(1, b, n_mem, MEM_HEADS, MEM_HD),
        st_s[None],
        zs[:, OFF_DK:OFF_DK + DIFF_QK_W].reshape(1, db, ds, nh2, DIFF_HD),
        zs[:, OFF_DV:OFF_DV + DIFF_V_W].reshape(1, db, ds, DIFF_HEADS, 2 * DIFF_HD),
    )
```

```python
import functools
import math

import numpy as np
import jax
import jax.numpy as jnp
from jax import lax
from jax.experimental import pallas as pl
from jax.experimental.pallas import tpu as pltpu

f32 = jnp.float32
bf16 = jnp.bfloat16

D_MODEL = 2048
RET_HEADS = 8
RET_DK = 128
RET_DV = 256
RET_CHUNK = 128
ROPE_BASE = 10000.0
DIFF_HEADS = 8
DIFF_HD = 64
REL_BUCKETS = 32
REL_MAX_DIST = 128
MEM_HEADS = 4
MEM_HD = 128
PEER_HEADS = 8
PEER_NKEYS = 128
PEER_TOPK = 16
PEER_DK = 128
EPS = 1e-6
NEG = -1e30

RET_QK_W = RET_HEADS * RET_DK
RET_V_W = RET_HEADS * RET_DV
DIFF_QK_W = 2 * DIFF_HEADS * DIFF_HD
DIFF_V_W = DIFF_HEADS * 2 * DIFF_HD
MEM_W = MEM_HEADS * MEM_HD
OFF_RQ = 0
OFF_RK = OFF_RQ + RET_QK_W
OFF_RV = OFF_RK + RET_QK_W
OFF_RG = OFF_RV + RET_V_W
OFF_DQ = OFF_RG + RET_V_W
OFF_DK = OFF_DQ + DIFF_QK_W
OFF_DV = OFF_DK + DIFF_QK_W
OFF_MQ = OFF_DV + DIFF_V_W
OFF_GA = OFF_MQ + MEM_W
OFF_GB = OFF_GA + D_MODEL
OFF_GC = OFF_GB + D_MODEL
IN_WIDTH = OFF_GC + D_MODEL

VMEM_LIMIT_BYTES = 56 * 1024 * 1024

NT = (((1,), (1,)), ((), ()))
TN = (((0,), (0,)), ((), ()))


def _cp(*sem):
    return pltpu.CompilerParams(dimension_semantics=sem, vmem_limit_bytes=VMEM_LIMIT_BYTES)


def _dot(a, b):
    return jnp.dot(a, b, preferred_element_type=f32)


def _dot_nt(a, b):
    return lax.dot_general(a, b, NT, preferred_element_type=f32)


def _dot_tn(a, b):
    return lax.dot_general(a, b, TN, preferred_element_type=f32)


def _rms_rows(x, g):
    y = x * lax.rsqrt(jnp.mean(x * x, axis=-1, keepdims=True) + EPS)
    return y * g


def _norm_matmul_kernel(x_ref, g_ref, w_ref, o_ref, *rest, rows):
    xn_ref = rest[-1]
    xn_out_ref = rest[0] if len(rest) == 2 else None

    @pl.when(pl.program_id(1) == 0)
    def _():
        def body(r, c):
            sl = pl.ds(pl.multiple_of(r * rows, rows), rows)
            xn = _rms_rows(x_ref[sl, :], g_ref[...]).astype(bf16)
            xn_ref[sl, :] = xn
            if xn_out_ref is not None:
                xn_out_ref[sl, :] = xn
            return c
        lax.fori_loop(0, x_ref.shape[0] // rows, body, 0)

    o_ref[...] = _dot(xn_ref[...], w_ref[...])


def norm_matmul(x, g, w, *, tm=1024, tn=512, with_rows=False):
    t, d = x.shape
    n = w.shape[1]
    tm = min(tm, t)
    assert t % tm == 0 and n % tn == 0 and tm % 16 == 0
    rows = 32 if tm % 32 == 0 else 16
    out_specs = [pl.BlockSpec((tm, tn), lambda i, j: (i, j))]
    out_shape = [jax.ShapeDtypeStruct((t, n), f32)]
    if with_rows:
        out_specs.append(pl.BlockSpec((tm, d), lambda i, j: (i, 0)))
        out_shape.append(jax.ShapeDtypeStruct((t, d), bf16))
    out = pl.pallas_call(
        functools.partial(_norm_matmul_kernel, rows=rows),
        grid=(t // tm, n // tn),
        in_specs=[
            pl.BlockSpec((tm, d), lambda i, j: (i, 0), pipeline_mode=pl.Buffered(1)),
            pl.BlockSpec((1, d), lambda i, j: (0, 0)),
            pl.BlockSpec((d, tn), lambda i, j: (0, j)),
        ],
        out_specs=out_specs,
        out_shape=out_shape,
        scratch_shapes=[pltpu.VMEM((tm, d), bf16)],
        compiler_params=_cp("parallel", "arbitrary"),
        name="norm_matmul",
    )(x, g.reshape(1, d), w)
    return out if with_rows else out[0]


def _ret_log_gamma():
    return jnp.log1p(-jnp.exp2(-5.0 - jnp.arange(RET_HEADS, dtype=f32)))


def _rope_tables(pos):
    half = RET_DK // 2
    inv = 1.0 / (ROPE_BASE ** jnp.linspace(0.0, 1.0, half, dtype=f32))
    ang = pos.astype(f32)[:, None] * inv[None, :]
    c, s = jnp.cos(ang), jnp.sin(ang)
    return jnp.concatenate([c, c], axis=-1), jnp.concatenate([-s, s], axis=-1)


def _rope(x, cos2, sin2):
    return x * cos2 + pltpu.roll(x, RET_DK // 2, 1) * sin2


def _ret_out(o, g):
    on = o * lax.rsqrt(jnp.mean(o * o, axis=-1, keepdims=True) + EPS)
    return (on * (g * jax.nn.sigmoid(g))).astype(bf16)


def _ret_prompt_kernel(decay_ref, q_ref, k_ref, v_ref, g_ref, cos_ref, sin_ref, dmask_ref, cs_ref, kdec_ref,
                       o_ref, st_out_ref, st_ref):
    c = pl.program_id(1)

    @pl.when(c == 0)
    def _():
        st_ref[...] = jnp.zeros_like(st_ref)

    cos2 = cos_ref[...]
    sin2 = sin_ref[...]
    for h in range(RET_HEADS):
        ks = slice(h * RET_DK, (h + 1) * RET_DK)
        vs = slice(h * RET_DV, (h + 1) * RET_DV)
        q = _rope(q_ref[:, ks], cos2, sin2)
        k = _rope(k_ref[:, ks], cos2, sin2) * (RET_DK ** -0.5)
        v = v_ref[:, vs]
        st = st_ref[h]
        qk = _dot_nt(q, k) * dmask_ref[h]
        o = _dot(qk, v) + _dot(q, st) * cs_ref[h]
        st_ref[h] = decay_ref[h] * st + _dot_tn(k * kdec_ref[h], v)
        o_ref[:, vs] = _ret_out(o, g_ref[:, vs])

    @pl.when(c == pl.num_programs(1) - 1)
    def _():
        st_out_ref[...] = st_ref[...]


def retention_prompt(z, batch, seq):
    cl = RET_CHUNK
    assert seq % cl == 0
    nc = seq // cl
    lg = _ret_log_gamma()[:, None]
    idx = jnp.arange(cl, dtype=f32)
    rel = idx[:, None] - idx[None, :]
    dmask = jnp.where(rel >= 0, jnp.exp(lg[:, :, None] * jnp.maximum(rel, 0.0)), 0.0)
    cs = jnp.broadcast_to(jnp.exp(lg * (idx + 1.0))[:, :, None], (RET_HEADS, cl, RET_DV))
    kdec = jnp.broadcast_to(jnp.exp(lg * (cl - 1.0 - idx))[:, :, None], (RET_HEADS, cl, RET_DK))
    decay = jnp.exp(lg[:, 0] * cl)
    cos2, sin2 = _rope_tables(jnp.arange(seq, dtype=jnp.int32))
    row = lambda b, c: b * nc + c
    const3 = lambda b, c: (0, 0, 0)
    return pl.pallas_call(
        _ret_prompt_kernel,
        grid=(batch, nc),
        in_specs=[
            pl.BlockSpec(memory_space=pltpu.SMEM),
            pl.BlockSpec((cl, RET_QK_W), lambda b, c: (row(b, c), OFF_RQ // RET_QK_W)),
            pl.BlockSpec((cl, RET_QK_W), lambda b, c: (row(b, c), OFF_RK // RET_QK_W)),
            pl.BlockSpec((cl, RET_V_W), lambda b, c: (row(b, c), OFF_RV // RET_V_W)),
            pl.BlockSpec((cl, RET_V_W), lambda b, c: (row(b, c), OFF_RG // RET_V_W)),
            pl.BlockSpec((cl, RET_DK), lambda b, c: (c, 0)),
            pl.BlockSpec((cl, RET_DK), lambda b, c: (c, 0)),
            pl.BlockSpec((RET_HEADS, cl, cl), const3),
            pl.BlockSpec((RET_HEADS, cl, RET_DV), const3),
            pl.BlockSpec((RET_HEADS, cl, RET_DK), const3),
        ],
        out_specs=[
            pl.BlockSpec((cl, RET_V_W), lambda b, c: (row(b, c), 0)),
            pl.BlockSpec((None, RET_HEADS, RET_DK, RET_DV), lambda b, c: (b, 0, 0, 0)),
        ],
        out_shape=[
            jax.ShapeDtypeStruct((batch * seq, RET_V_W), bf16),
            jax.ShapeDtypeStruct((batch, RET_HEADS, RET_DK, RET_DV), f32),
        ],
        scratch_shapes=[pltpu.VMEM((RET_HEADS, RET_DK, RET_DV), f32)],
        compiler_params=_cp("parallel", "arbitrary"),
        name="retention_prompt",
    )(decay, z, z, z, z, cos2, sin2, dmask, cs, kdec)


RET_SAMPLE_GROUP = 4


def _ret_sample_kernel(decay_ref, q_ref, k_ref, v_ref, g_ref, st_in_ref, cos_ref, sin_ref, dmask_ref, cs_ref,
                       kdec_ref, o_ref, st_out_ref, *, ds):
    cos2 = cos_ref[...]
    sin2 = sin_ref[...]
    rows = q_ref.shape[0]
    seq_of_row = lax.broadcasted_iota(jnp.int32, (rows, RET_DV), 0) // ds
    for h in range(RET_HEADS):
        ks = slice(h * RET_DK, (h + 1) * RET_DK)
        vs = slice(h * RET_DV, (h + 1) * RET_DV)
        q = _rope(q_ref[:, ks], cos2, sin2)
        k = _rope(k_ref[:, ks], cos2, sin2) * (RET_DK ** -0.5)
        v = v_ref[:, vs]
        kd = k * kdec_ref[h]
        qk = _dot_nt(q, k) * dmask_ref[h]
        cross = jnp.zeros((rows, RET_DV), f32)
        for s in range(RET_SAMPLE_GROUP):
            st = st_in_ref[s, h]
            cross = jnp.where(seq_of_row == s, _dot(q, st), cross)
            vb = jnp.where(seq_of_row == s, v, 0.0)
            st_out_ref[s, h] = decay_ref[h] * st + _dot_tn(kd, vb)
        o = _dot(qk, v) + cross * cs_ref[h]
        o_ref[:, vs] = _ret_out(o, g_ref[:, vs])


def retention_sample(z, state0, n_seq, ds, past):
    g = RET_SAMPLE_GROUP
    assert n_seq % g == 0 and (g * ds) % 8 == 0 and (ds % RET_CHUNK != 0)
    rows = g * ds
    lg = _ret_log_gamma()[:, None]
    idx = jnp.arange(ds, dtype=f32)
    rel = idx[:, None] - idx[None, :]
    dm = jnp.where(rel >= 0, jnp.exp(lg[:, :, None] * jnp.maximum(rel, 0.0)), 0.0)
    same_seq = jnp.kron(jnp.eye(g, dtype=f32), jnp.ones((ds, ds), f32))
    dmask = jnp.tile(dm, (1, g, g)) * same_seq[None]
    cs = jnp.broadcast_to(jnp.tile(jnp.exp(lg * (idx + 1.0)), (1, g))[:, :, None], (RET_HEADS, rows, RET_DV))
    kdec = jnp.broadcast_to(jnp.tile(jnp.exp(lg * (ds - 1.0 - idx)), (1, g))[:, :, None], (RET_HEADS, rows, RET_DK))
    decay = jnp.exp(lg[:, 0] * ds)
    cos2, sin2 = _rope_tables(past + jnp.arange(ds, dtype=jnp.int32))
    cos2, sin2 = jnp.tile(cos2, (g, 1)), jnp.tile(sin2, (g, 1))
    const2 = lambda i: (0, 0)
    const3 = lambda i: (0, 0, 0)
    st_spec = pl.BlockSpec((g, RET_HEADS, RET_DK, RET_DV), lambda i: (i, 0, 0, 0))
    return pl.pallas_call(
        functools.partial(_ret_sample_kernel, ds=ds),
        grid=(n_seq // g,),
        in_specs=[
            pl.BlockSpec(memory_space=pltpu.SMEM),
            pl.BlockSpec((rows, RET_QK_W), lambda i: (i, OFF_RQ // RET_QK_W)),
            pl.BlockSpec((rows, RET_QK_W), lambda i: (i, OFF_RK // RET_QK_W)),
            pl.BlockSpec((rows, RET_V_W), lambda i: (i, OFF_RV // RET_V_W)),
            pl.BlockSpec((rows, RET_V_W), lambda i: (i, OFF_RG // RET_V_W)),
            st_spec,
            pl.BlockSpec((rows, RET_DK), const2),
            pl.BlockSpec((rows, RET_DK), const2),
            pl.BlockSpec((RET_HEADS, rows, rows), const3),
            pl.BlockSpec((RET_HEADS, rows, RET_DV), const3),
            pl.BlockSpec((RET_HEADS, rows, RET_DK), const3),
        ],
        out_specs=[pl.BlockSpec((rows, RET_V_W), lambda i: (i, 0)), st_spec],
        out_shape=[
            jax.ShapeDtypeStruct((n_seq * ds, RET_V_W), bf16),
            jax.ShapeDtypeStruct((n_seq, RET_HEADS, RET_DK, RET_DV), f32),
        ],
        compiler_params=_cp("parallel"),
        name="retention_sample",
    )(decay, z, z, z, z, state0, cos2, sin2, dmask, cs, kdec)


def _lambda_init(layer):
    return 0.8 - 0.6 * math.exp(-0.3 * layer)


def _lambda_kernel(q1_ref, k1_ref, q2_ref, k2_ref, o_ref, *, lam_init):
    a = jnp.sum(q1_ref[...] * k1_ref[...], axis=-1, keepdims=True)
    b = jnp.sum(q2_ref[...] * k2_ref[...], axis=-1, keepdims=True)
    o_ref[...] = jnp.broadcast_to(jnp.exp(a) - jnp.exp(b) + lam_init, o_ref.shape)


def diff_lambda(q1, k1, q2, k2, lam_init):
    r = lambda a: a.reshape(1, DIFF_HD)
    return pl.pallas_call(
        functools.partial(_lambda_kernel, lam_init=lam_init),
        out_shape=jax.ShapeDtypeStruct((1, 2 * DIFF_HD), f32),
        name="diff_lambda",
    )(r(q1), r(k1), r(q2), r(k2))


def _rel_bucket_np(rel):
    n = np.maximum(rel, 0)
    max_exact = REL_BUCKETS // 2
    nf = np.maximum(n, 1).astype(np.float32)
    large = max_exact + (np.log(nf / np.float32(max_exact)) / np.float32(math.log(REL_MAX_DIST / max_exact))
                         * np.float32(REL_BUCKETS - max_exact)).astype(np.int32)
    large = np.minimum(large, REL_BUCKETS - 1)
    return np.where(n < max_exact, n, large).astype(np.int32)


def _bias_expand_kernel(rb_ref, tbl_ref, o_ref):
    h = pl.program_id(0)
    tbl = tbl_ref[...]
    acc = jnp.full(tbl.shape, NEG, f32)
    for bkt in range(REL_BUCKETS):
        acc = jnp.where(tbl == bkt, rb_ref[bkt, h], acc)
    o_ref[...] = acc


def bias_expand(bucket_tbl, rel_bias):
    r, c = bucket_tbl.shape
    return pl.pallas_call(
        _bias_expand_kernel,
        grid=(DIFF_HEADS,),
        in_specs=[pl.BlockSpec(memory_space=pltpu.SMEM), pl.BlockSpec((r, c), lambda h: (0, 0))],
        out_specs=pl.BlockSpec((None, r, c), lambda h: (h, 0, 0)),
        out_shape=jax.ShapeDtypeStruct((DIFF_HEADS, r, c), f32),
        compiler_params=_cp("parallel"),
        name="bias_expand",
    )(rel_bias.astype(f32), bucket_tbl)


def _subln_out(o, w, lam_init):
    on = o * lax.rsqrt(jnp.mean(o * o, axis=-1, keepdims=True) + EPS)
    return (on * w * (1.0 - lam_init)).astype(bf16)


def _softmax_step(s, v, m_old, l_old, acc_old):
    m_new = jnp.maximum(m_old, jnp.max(s, axis=-1, keepdims=True))
    p = jnp.exp(s - m_new)
    alpha = jnp.exp(m_old - m_new)
    return m_new, alpha * l_old + jnp.sum(p, axis=-1, keepdims=True), alpha * acc_old + _dot(p, v)


DIFF_TILE = 256


def _diff_prompt_kernel(lam_ref, q_ref, k_ref, v_ref, bias_ref, w_ref, o_ref, vt_ref, m_ref, l_ref, acc_ref, *,
                        lam_init):
    qi = pl.program_id(2)
    t = DIFF_TILE

    @pl.when(qi == 0)
    def _():
        vt_ref[...] = v_ref[...].T.astype(bf16)

    q = q_ref[...] * (DIFF_HD ** -0.5)
    lane = lax.broadcasted_iota(jnp.int32, q.shape, 1)
    qm = (jnp.where(lane < DIFF_HD, q, 0.0).astype(bf16), jnp.where(lane >= DIFF_HD, q, 0.0).astype(bf16))
    m_ref[...] = jnp.full(m_ref.shape, NEG, f32)
    l_ref[...] = jnp.zeros_like(l_ref)
    acc_ref[...] = jnp.zeros_like(acc_ref)

    def body(kj, carry):
        sl = pl.ds(pl.multiple_of(kj * t, t), t)
        kb = k_ref[sl, :].astype(bf16)
        vtb = vt_ref[:, sl]
        bias = bias_ref[jnp.minimum(qi - kj, 2)]
        for mp in range(2):
            s = _dot_nt(kb, qm[mp]) + bias
            m_old = m_ref[mp]
            m_new = jnp.maximum(m_old, jnp.max(s, axis=0, keepdims=True))
            p = jnp.exp(s - m_new)
            alpha = jnp.exp(m_old - m_new)
            l_ref[mp] = alpha * l_ref[mp] + jnp.sum(p, axis=0, keepdims=True)
            acc_ref[mp] = alpha * acc_ref[mp] + _dot(vtb, p.astype(bf16))
            m_ref[mp] = m_new
        return carry

    lax.fori_loop(0, qi + 1, body, 0)
    o0 = (acc_ref[0] / l_ref[0]).T
    o1 = (acc_ref[1] / l_ref[1]).T
    o_ref[...] = _subln_out(o0 - lam_ref[...] * o1, w_ref[...], lam_init)


def diff_attn_prompt(z, lam, rel_bias, subln_w, batch, seq, lam_init):
    t = DIFF_TILE
    assert seq % t == 0
    nq = seq // t
    i = np.arange(t)
    rel0 = i[None, :] - i[:, None]
    tb0 = np.where(rel0 >= 0, _rel_bucket_np(rel0), -1)
    tb1 = _rel_bucket_np(rel0 + t)
    far = _rel_bucket_np(np.arange(t + 1, max(seq, 2 * t + 1)))
    assert (far == far[0]).all()
    tb2 = np.full((t, t), far[0], np.int32)
    tbl = jnp.asarray(np.concatenate([tb0, tb1, tb2], axis=0).astype(np.int32))
    bias = bias_expand(tbl, rel_bias).reshape(DIFF_HEADS, 3, t, t)
    w2 = 2 * DIFF_HD
    return pl.pallas_call(
        functools.partial(_diff_prompt_kernel, lam_init=lam_init),
        grid=(batch, DIFF_HEADS, nq),
        in_specs=[
            pl.BlockSpec((1, w2), lambda b, h, i: (0, 0)),
            pl.BlockSpec((t, w2), lambda b, h, i: (b * nq + i, OFF_DQ // w2 + h)),
            pl.BlockSpec((seq, w2), lambda b, h, i: (b, OFF_DK // w2 + h)),
            pl.BlockSpec((seq, w2), lambda b, h, i: (b, OFF_DV // w2 + h)),
            pl.BlockSpec((None, 3, t, t), lambda b, h, i: (h, 0, 0, 0)),
            pl.BlockSpec((1, w2), lambda b, h, i: (0, 0)),
        ],
        out_specs=pl.BlockSpec((t, w2), lambda b, h, i: (b * nq + i, h)),
        out_shape=jax.ShapeDtypeStruct((batch * seq, DIFF_V_W), bf16),
        scratch_shapes=[pltpu.VMEM((w2, seq), bf16), pltpu.VMEM((2, 1, t), f32), pltpu.VMEM((2, 1, t), f32),
                        pltpu.VMEM((2, w2, t), f32)],
        compiler_params=_cp("parallel", "parallel", "arbitrary"),
        name="diff_attn_prompt",
    )(lam, z, z, z, bias, subln_w.reshape(1, w2).astype(f32))


DIFF_SAMPLE_PAGES = 4


def _diff_sample_kernel(pt_ref, lam_ref, q_ref, kn_ref, vn_ref, *rest, ds, pg, lam_init):
    k_refs, v_refs = rest[:pg], rest[pg:2 * pg]
    bias_ref, biasn_ref, w_ref, o_ref, qbd_ref, kpad_ref, vpad_ref, m_ref, l_ref, acc_ref = rest[2 * pg:]
    b = pl.program_id(0)
    p = pl.program_id(1)
    w2 = 2 * DIFF_HD
    hr = 2 * ds

    @pl.when((b == 0) & (p == 0))
    def _():
        kpad_ref[...] = jnp.zeros_like(kpad_ref)
        vpad_ref[...] = jnp.zeros_like(vpad_ref)

    @pl.when(p == 0)
    def _():
        lane = lax.broadcasted_iota(jnp.int32, (ds, w2), 1)
        for h in range(DIFF_HEADS):
            qh = q_ref[:, h * w2:(h + 1) * w2] * (DIFF_HD ** -0.5)
            qbd_ref[h * hr:h * hr + ds, :] = jnp.where(lane < DIFF_HD, qh, 0.0)
            qbd_ref[h * hr + ds:(h + 1) * hr, :] = jnp.where(lane >= DIFF_HD, qh, 0.0)
        kpad_ref[0:ds, :] = kn_ref[...]
        vpad_ref[0:ds, :] = vn_ref[...]
        m_ref[...] = jnp.full(m_ref.shape, NEG, f32)
        l_ref[...] = jnp.zeros_like(l_ref)
        acc_ref[...] = jnp.zeros_like(acc_ref)

    def process(h, kh, vh, bias):
        rs = slice(h * hr, (h + 1) * hr)
        s = _dot_nt(qbd_ref[rs, :], kh) + bias
        m_ref[rs, :], l_ref[rs, :], acc_ref[rs, :] = _softmax_step(s, vh, m_ref[rs, :], l_ref[rs, :], acc_ref[rs, :])

    for h in range(DIFF_HEADS):
        kh = jnp.concatenate(
            [jnp.concatenate([k_refs[i][:, 2 * h, :], k_refs[i][:, 2 * h + 1, :]], axis=-1) for i in range(pg)], axis=0)
        vh = jnp.concatenate([v_refs[i][:, h, :] for i in range(pg)], axis=0)
        process(h, kh, vh, bias_ref[p, h * hr:(h + 1) * hr, :])

    @pl.when(p == pl.num_programs(1) - 1)
    def _():
        for h in range(DIFF_HEADS):
            hs = slice(h * w2, (h + 1) * w2)
            process(h, kpad_ref[:, hs], vpad_ref[:, hs], biasn_ref[h * hr:(h + 1) * hr, :])
            rs = slice(h * hr, (h + 1) * hr)
            blk = acc_ref[rs, :] / l_ref[rs, :]
            o = blk[0:ds] - lam_ref[...] * blk[ds:hr]
            o_ref[:, hs] = _subln_out(o, w_ref[...], lam_init)


def diff_attn_sample(z3, cache_k, cache_v, page_table, lam, rel_bias, subln_w, lam_init):
    n_seq, ds, _ = z3.shape
    n_pages = page_table.shape[1]
    page = cache_k.shape[1]
    past = n_pages * page
    rows = 2 * DIFF_HEADS * ds
    pg = math.gcd(n_pages, DIFF_SAMPLE_PAGES)
    n_steps = n_pages // pg
    assert page == 128 and 2 * ds == 8
    qi = np.arange(ds)
    rel = past + qi[None, :, None] - np.arange(past).reshape(n_steps, 1, pg * page)
    tb = np.broadcast_to(_rel_bucket_np(rel)[:, None], (n_steps, 2, ds, pg * page))
    rel_new = qi[:, None] - qi[None, :]
    tbn = np.full((2, ds, page), -1, np.int32)
    tbn[:, :, :ds] = np.where(rel_new >= 0, _rel_bucket_np(rel_new), -1)[None]
    bias = bias_expand(jnp.asarray(tb.reshape(n_steps * 2 * ds, pg * page)), rel_bias)
    bias = bias.reshape(DIFF_HEADS, n_steps, 2 * ds, pg * page).transpose(1, 0, 2, 3).reshape(n_steps, rows, pg * page)
    biasn = bias_expand(jnp.asarray(tbn.reshape(2 * ds, page)), rel_bias).reshape(rows, page)
    w2 = 2 * DIFF_HD
    page_spec = lambda shape, i: pl.BlockSpec(
        (None,) + shape, lambda b, p, pt: (pt[b * n_pages + p * pg + i], 0, 0, 0))
    grid_spec = pltpu.PrefetchScalarGridSpec(
        num_scalar_prefetch=1,
        grid=(n_seq, n_steps),
        in_specs=[
            pl.BlockSpec((1, w2), lambda b, p, pt: (0, 0)),
            pl.BlockSpec((None, ds, DIFF_QK_W), lambda b, p, pt: (b, 0, OFF_DQ // DIFF_QK_W)),
            pl.BlockSpec((None, ds, DIFF_QK_W), lambda b, p, pt: (b, 0, OFF_DK // DIFF_QK_W)),
            pl.BlockSpec((None, ds, DIFF_V_W), lambda b, p, pt: (b, 0, OFF_DV // DIFF_V_W)),
            *[page_spec(cache_k.shape[1:], i) for i in range(pg)],
            *[page_spec(cache_v.shape[1:], i) for i in range(pg)],
            pl.BlockSpec((n_steps, rows, pg * page), lambda b, p, pt: (0, 0, 0)),
            pl.BlockSpec((rows, page), lambda b, p, pt: (0, 0)),
            pl.BlockSpec((1, w2), lambda b, p, pt: (0, 0)),
        ],
        out_specs=pl.BlockSpec((None, ds, DIFF_V_W), lambda b, p, pt: (b, 0, 0)),
        scratch_shapes=[
            pltpu.VMEM((rows, w2), f32),
            pltpu.VMEM((page, DIFF_QK_W), f32),
            pltpu.VMEM((page, DIFF_V_W), f32),
            pltpu.VMEM((rows, 1), f32),
            pltpu.VMEM((rows, 1), f32),
            pltpu.VMEM((rows, w2), f32),
        ],
    )
    return pl.pallas_call(
        functools.partial(_diff_sample_kernel, ds=ds, pg=pg, lam_init=lam_init),
        grid_spec=grid_spec,
        out_shape=jax.ShapeDtypeStruct((n_seq, ds, DIFF_V_W), bf16),
        compiler_params=_cp("arbitrary", "arbitrary"),
        name="diff_attn_sample",
    )(page_table.reshape(-1), lam, z3, z3, z3, *([cache_k] * pg), *([cache_v] * pg), bias, biasn,
      subln_w.reshape(1, w2).astype(f32))


def _mem_attn_head(q, mk, mv):
    s = _dot_nt(q, mk) * (MEM_HD ** -0.5)
    m = jnp.max(s, axis=-1, keepdims=True)
    p = jnp.exp(s - m)
    return (_dot(p, mv) / jnp.sum(p, axis=-1, keepdims=True)).astype(bf16)


def _mem_attn_kernel(q_ref, mk_ref, mv_ref, o_ref):
    for h in range(MEM_HEADS):
        hs = slice(h * MEM_HD, (h + 1) * MEM_HD)
        o_ref[:, hs] = _mem_attn_head(q_ref[:, hs], mk_ref[:, hs], mv_ref[:, hs])


def _mem_attn_sample_kernel(q_ref, mk_ref, mv_ref, o_ref):
    for h in range(MEM_HEADS):
        hs = slice(h * MEM_HD, (h + 1) * MEM_HD)
        o_ref[:, hs] = _mem_attn_head(q_ref[:, hs], mk_ref[:, h, :], mv_ref[:, h, :])


def mem_attn_prompt(z, mkv, batch, seq, n_mem, *, tq=256):
    assert seq % tq == 0
    nq = seq // tq
    return pl.pallas_call(
        _mem_attn_kernel,
        grid=(batch, nq),
        in_specs=[
            pl.BlockSpec((tq, MEM_W), lambda b, i: (b * nq + i, OFF_MQ // MEM_W)),
            pl.BlockSpec((n_mem, MEM_W), lambda b, i: (b, 0)),
            pl.BlockSpec((n_mem, MEM_W), lambda b, i: (b, 1)),
        ],
        out_specs=pl.BlockSpec((tq, MEM_W), lambda b, i: (b * nq + i, 0)),
        out_shape=jax.ShapeDtypeStruct((batch * seq, MEM_W), bf16),
        compiler_params=_cp("parallel", "parallel"),
        name="mem_attn_prompt",
    )(z, mkv, mkv)


def mem_attn_sample(z3, mk, mv):
    n_seq, ds, _ = z3.shape
    n_mem = mk.shape[1]
    mem_spec = pl.BlockSpec((None, n_mem, MEM_HEADS, MEM_HD), lambda b: (b, 0, 0, 0))
    return pl.pallas_call(
        _mem_attn_sample_kernel,
        grid=(n_seq,),
        in_specs=[pl.BlockSpec((None, ds, MEM_W), lambda b: (b, 0, OFF_MQ // MEM_W)), mem_spec, mem_spec],
        out_specs=pl.BlockSpec((None, ds, MEM_W), lambda b: (b, 0, 0)),
        out_shape=jax.ShapeDtypeStruct((n_seq, ds, MEM_W), bf16),
        compiler_params=_cp("parallel"),
        name="mem_attn_sample",
    )(z3, mk, mv)


def _mix_kernel(orr_ref, od_ref, om_ref, ga_ref, gb_ref, gc_ref, wr_ref, wd_ref, wm_ref, o_ref):
    mixed = (jax.nn.sigmoid(ga_ref[...]) * _dot(orr_ref[...], wr_ref[...])
             + jax.nn.sigmoid(gb_ref[...]) * _dot(od_ref[...], wd_ref[...])
             + jax.nn.sigmoid(gc_ref[...]) * _dot(om_ref[...], wm_ref[...]))
    o_ref[...] = mixed.astype(bf16)


def gated_mix(o_r, o_d, o_m, z, w_r, w_d, w_m, *, tm=512, tn=512):
    t = o_r.shape[0]
    tm = min(tm, t)
    assert t % tm == 0 and D_MODEL % tn == 0
    gate = lambda off: pl.BlockSpec((tm, tn), lambda i, j: (i, off // tn + j))
    act = lambda w: pl.BlockSpec((tm, w), lambda i, j: (i, 0))
    wgt = lambda w: pl.BlockSpec((w, tn), lambda i, j: (0, j))
    return pl.pallas_call(
        _mix_kernel,
        grid=(t // tm, D_MODEL // tn),
        in_specs=[act(RET_V_W), act(DIFF_V_W), act(MEM_W), gate(OFF_GA), gate(OFF_GB), gate(OFF_GC),
                  wgt(RET_V_W), wgt(DIFF_V_W), wgt(MEM_W)],
        out_specs=pl.BlockSpec((tm, tn), lambda i, j: (i, j)),
        out_shape=jax.ShapeDtypeStruct((t, D_MODEL), bf16),
        compiler_params=_cp("parallel", "parallel"),
        name="gated_mix",
    )(o_r, o_d, o_m, z, z, z, w_r, w_d, w_m)


def _matmul_residual_kernel(x_ref, a_ref, w_ref, o_ref):
    o_ref[...] = x_ref[...] + _dot(a_ref[...], w_ref[...])


def matmul_residual(x, a, w, *, tm=512, tn=512):
    t, k = a.shape
    n = w.shape[1]
    tm = min(tm, t)
    assert t % tm == 0 and n % tn == 0
    return pl.pallas_call(
        _matmul_residual_kernel,
        grid=(t // tm, n // tn),
        in_specs=[
            pl.BlockSpec((tm, tn), lambda i, j: (i, j)),
            pl.BlockSpec((tm, k), lambda i, j: (i, 0)),
            pl.BlockSpec((k, tn), lambda i, j: (0, j)),
        ],
        out_specs=pl.BlockSpec((tm, tn), lambda i, j: (i, j)),
        out_shape=jax.ShapeDtypeStruct((t, n), f32),
        compiler_params=_cp("parallel", "parallel"),
        name="matmul_residual",
    )(x, a, w)


_PEER_CAND = [(a, b) for a in range(PEER_TOPK) for b in range(PEER_TOPK) if (a + 1) * (b + 1) <= PEER_TOPK]


def _top_values(x, k):
    out = []
    for i in range(k):
        mx = jnp.max(x, axis=0, keepdims=True)
        out.append(mx)
        if i + 1 < k:
            x = jnp.where(x == mx, -jnp.inf, x)
    return out


def _peer_select_kernel(pq_ref, keys_ref, a1_ref, a2_ref, e1_ref, e2_ref, tau_ref):
    for h in range(PEER_HEADS):
        sc = []
        for c in range(2):
            off = (2 * h + c) * PEER_DK
            sc.append(_dot_nt(keys_ref[h, c], pq_ref[:, off:off + PEER_DK]))
        a1_ref[h] = sc[0]
        a2_ref[h] = sc[1]
        t1 = _top_values(sc[0], PEER_TOPK)
        t2 = _top_values(sc[1], PEER_TOPK)
        cand = jnp.concatenate([t1[a] + t2[b] for a, b in _PEER_CAND], axis=0)
        top = _top_values(cand, PEER_TOPK)
        m = top[0]
        zsum = jnp.ones_like(m)
        for r in range(1, PEER_TOPK):
            zsum = zsum + jnp.exp(top[r] - m)
        tau_ref[h:h + 1, :] = top[PEER_TOPK - 1]
        e1_ref[h] = jnp.exp(sc[0] - t1[0]) / zsum
        e2_ref[h] = jnp.exp(sc[1] - t2[0])


def peer_select(pq, keys, *, tt=256):
    t = pq.shape[0]
    tt = min(tt, t)
    assert t % tt == 0
    big = pl.BlockSpec((PEER_HEADS, PEER_NKEYS, tt), lambda i: (0, 0, i))
    shp = jax.ShapeDtypeStruct((PEER_HEADS, PEER_NKEYS, t), f32)
    return pl.pallas_call(
        _peer_select_kernel,
        grid=(t // tt,),
        in_specs=[
            pl.BlockSpec((tt, PEER_HEADS * 2 * PEER_DK), lambda i: (i, 0)),
            pl.BlockSpec((PEER_HEADS, 2, PEER_NKEYS, PEER_DK), lambda i: (0, 0, 0, 0)),
        ],
        out_specs=[big, big, big, big, pl.BlockSpec((PEER_HEADS, tt), lambda i: (0, i))],
        out_shape=[shp, shp, shp, shp, jax.ShapeDtypeStruct((PEER_HEADS, t), f32)],
        compiler_params=_cp("parallel"),
        name="peer_select",
    )(pq, keys)


PEER_ROWS = 8


def _gelu(x):
    return 0.5 * x * (1.0 + lax.erf(x * (2.0 ** -0.5)))


PEER_SUB = 16


def _peer_weights_block(c0, i0, ht_ref, wt_ref, a1_ref, e1_ref, a2_ref, e2_ref, tau_ref):
    cs = slice(c0, c0 + 128)
    isl = slice(i0, i0 + PEER_SUB)
    g = [None] * PEER_ROWS
    for h in range(PEER_HEADS):
        a2 = a2_ref[h, isl, cs]
        e2 = e2_ref[h, isl, cs]
        tau = tau_ref[h:h + 1, cs]
        for r in range(PEER_ROWS):
            sc = a1_ref[h, r:r + 1, cs] + a2
            w = e1_ref[h, r:r + 1, cs] * e2
            sel = jnp.where(sc >= tau, w, 0.0)
            g[r] = sel if g[r] is None else g[r] + sel
    for r in range(PEER_ROWS):
        rs = slice(r * PEER_NKEYS + i0, r * PEER_NKEYS + i0 + PEER_SUB)
        wt_ref[rs, cs] = (g[r] * _gelu(ht_ref[rs, cs])).astype(bf16)


def _spread(n_items, n_slots):
    return [range(-(-s * n_items // n_slots), -(-(s + 1) * n_items // n_slots)) for s in range(n_slots)]


def _peer_dense_kernel(xn_ref, u_ref, vt_ref, a1_ref, e1_ref, a2_ref, e2_ref, tau_ref, o_ref,
                       ht0_ref, ht1_ref, wt0_ref, wt1_ref, acc_ref, *, nj):
    s = pl.program_id(0)
    j_out = lax.rem(s + (nj - 2), nj)

    @pl.when(s == 0)
    def _():
        for ref in (ht0_ref, ht1_ref, wt0_ref, wt1_ref, acc_ref):
            ref[...] = jnp.zeros_like(ref)

    @pl.when((s >= 2) & (j_out == 0))
    def _():
        acc_ref[...] = jnp.zeros_like(acc_ref)

    ec, tt = ht0_ref.shape
    d = acc_ref.shape[0]
    mt = 256
    nt = min(256, tt)

    def step(ht_w, ht_r, wt_w, wt_r):
        def mm_scores(m0, n0):
            ht_w[m0:m0 + mt, n0:n0 + nt] = _dot_nt(u_ref[m0:m0 + mt, :], xn_ref[n0:n0 + nt, :])

        def mm_out(m0, n0):
            acc_ref[m0:m0 + mt, n0:n0 + nt] += _dot(vt_ref[m0:m0 + mt, :], wt_r[:, n0:n0 + nt])

        pieces = [(mm_scores, m0, n0) for m0 in range(0, ec, mt) for n0 in range(0, tt, nt)]
        pieces += [(mm_out, m0, n0) for m0 in range(0, d, mt) for n0 in range(0, tt, nt)]
        n1 = (ec // mt) * (tt // nt)
        pieces = [p for _, _, p in sorted(
            ((k + 0.5) / n1 if k < n1 else (k - n1 + 0.5) / (len(pieces) - n1), k, p)
            for k, p in enumerate(pieces))]
        blocks = [(c0, i0) for c0 in range(0, tt, 128) for i0 in range(0, PEER_NKEYS, PEER_SUB)]
        for (c0, i0), todo in zip(blocks, _spread(len(pieces), len(blocks))):
            for k in todo:
                fn, m0, n0 = pieces[k]
                fn(m0, n0)
            _peer_weights_block(c0, i0, ht_r, wt_w, a1_ref, e1_ref, a2_ref, e2_ref, tau_ref)

    even = lax.rem(s, 2) == 0

    @pl.when(even)
    def _():
        step(ht0_ref, ht1_ref, wt1_ref, wt0_ref)

    @pl.when(jnp.logical_not(even))
    def _():
        step(ht1_ref, ht0_ref, wt0_ref, wt1_ref)

    @pl.when((s >= 2) & (j_out == nj - 1))
    def _():
        o_ref[...] = acc_ref[...].T


def peer_dense(xn, u, vt, a1, a2, e1, e2, tau, *, tt=512):
    t, d = xn.shape
    n_exp = u.shape[0]
    tt = min(tt, t)
    ec = PEER_ROWS * PEER_NKEYS
    assert t % tt == 0 and n_exp % ec == 0 and tt % 128 == 0
    nj = n_exp // ec
    n = (t // tt) * nj
    assert nj > 2
    c_mm1 = lambda s: jnp.minimum(s, n - 1)
    c_wts = lambda s: jnp.clip(s - 1, 0, n - 1)
    c_mm2 = lambda s: jnp.clip(s - 2, 0, n - 1)
    once = pl.Buffered(1)
    row_blk = pl.BlockSpec((PEER_HEADS, PEER_ROWS, tt), lambda s: (0, c_wts(s) % nj, c_wts(s) // nj))
    all_blk = pl.BlockSpec((PEER_HEADS, PEER_NKEYS, tt), lambda s: (0, 0, c_wts(s) // nj), pipeline_mode=once)
    return pl.pallas_call(
        functools.partial(_peer_dense_kernel, nj=nj),
        grid=(n + 2,),
        in_specs=[
            pl.BlockSpec((tt, d), lambda s: (c_mm1(s) // nj, 0), pipeline_mode=once),
            pl.BlockSpec((ec, d), lambda s: (c_mm1(s) % nj, 0)),
            pl.BlockSpec((d, ec), lambda s: (0, c_mm2(s) % nj)),
            row_blk, row_blk, all_blk, all_blk,
            pl.BlockSpec((PEER_HEADS, tt), lambda s: (0, c_wts(s) // nj)),
        ],
        out_specs=pl.BlockSpec((tt, d), lambda s: (c_mm2(s) // nj, 0)),
        out_shape=jax.ShapeDtypeStruct((t, d), f32),
        scratch_shapes=[pltpu.VMEM((ec, tt), f32), pltpu.VMEM((ec, tt), f32), pltpu.VMEM((ec, tt), bf16),
                        pltpu.VMEM((ec, tt), bf16), pltpu.VMEM((d, tt), f32)],
        compiler_params=_cp("arbitrary"),
        name="peer_dense",
    )(xn, u, vt, a1, e1, a2, e2, tau)


def _final_norm_kernel(h_ref, p_ref, g_ref, o_ref):
    o_ref[...] = _rms_rows(h_ref[...] + p_ref[...], g_ref[...])


def final_norm(h, p, g, *, tm=256):
    t, d = h.shape
    tm = min(tm, t)
    assert t % tm == 0
    blk = pl.BlockSpec((tm, d), lambda i: (i, 0))
    return pl.pallas_call(
        _final_norm_kernel,
        grid=(t // tm,),
        in_specs=[blk, blk, pl.BlockSpec((1, d), lambda i: (0, 0))],
        out_specs=blk,
        out_shape=jax.ShapeDtypeStruct((t, d), f32),
        compiler_params=_cp("parallel"),
        name="final_norm",
    )(h, p, g.reshape(1, d))


def _token_tail(x2, z, o_r, o_d, o_m, wts):
    mixed = gated_mix(o_r, o_d, o_m, z, wts["w_up_ret"], wts["w_up_diff"], wts["w_up_mem"])
    h = matmul_residual(x2, mixed, wts["w_out"])
    pq, xn = norm_matmul(h, wts["norm_ffn_w"], wts["peer_w_q"], with_rows=True)
    a1, a2, e1, e2, tau = peer_select(pq, wts["peer_keys"])
    peer = peer_dense(xn, wts["peer_u"], wts["peer_vt"], a1, a2, e1, e2, tau)
    return final_norm(h, peer, wts["norm_final_w"])


def kernel(x_prompt, x_sample, mem_prompt, state_ret, cache_diff_k, cache_diff_v, cache_mem_k, cache_mem_v, page_table, norm_mix_w, norm_mem_w, norm_ffn_w, norm_final_w, w_in, w_mem_k, w_mem_v, diff_lambda_q1, diff_lambda_k1, diff_lambda_q2, diff_lambda_k2, diff_subln_w, rel_bias, w_up_ret, w_up_diff, w_up_mem, w_out, peer_w_q, peer_keys, peer_u, peer_v):
    b, s, d = x_prompt.shape
    db, ds, _ = x_sample.shape
    n_pages = page_table.shape[1]
    page = cache_diff_k.shape[2]
    past = n_pages * page
    n_mem = mem_prompt.shape[1]
    assert w_in.shape[0] == 1 and d == D_MODEL, "single-layer trunk"
    l = 0
    lam_init = _lambda_init(l)

    wts = {
        "w_up_ret": w_up_ret[l].astype(bf16), "w_up_diff": w_up_diff[l].astype(bf16),
        "w_up_mem": w_up_mem[l].astype(bf16), "w_out": w_out[l].astype(bf16),
        "norm_ffn_w": norm_ffn_w[l], "peer_w_q": peer_w_q[l].astype(bf16), "peer_keys": peer_keys[l],
        "peer_u": peer_u[l].astype(bf16), "peer_vt": peer_v[l].astype(bf16).T, "norm_final_w": norm_final_w,
    }
    w_in_b = w_in[l].astype(bf16)
    lam = diff_lambda(diff_lambda_q1[l], diff_lambda_k1[l], diff_lambda_q2[l], diff_lambda_k2[l], lam_init)

    xp = x_prompt.reshape(b * s, d)
    zp = norm_matmul(xp, norm_mix_w[l], w_in_b)
    w_mem = jnp.concatenate([w_mem_k[l], w_mem_v[l]], axis=1).astype(bf16)
    mkv = norm_matmul(mem_prompt.reshape(b * n_mem, d), norm_mem_w[l], w_mem)
    o_r, st_p = retention_prompt(zp, b, s)
    o_d = diff_attn_prompt(zp, lam, rel_bias, diff_subln_w[l], b, s, lam_init)
    o_m = mem_attn_prompt(zp, mkv, b, s, n_mem)
    y_prompt = _token_tail(xp, zp, o_r, o_d, o_m, wts).reshape(b, s, d)

    xs = x_sample.reshape(db * ds, d)
    zs = norm_matmul(xs, norm_mix_w[l], w_in_b)
    zs3 = zs.reshape(db, ds, IN_WIDTH)
    o_r_s, st_s = retention_sample(zs, state_ret[l], db, ds, past)
    o_d_s = diff_attn_sample(zs3, cache_diff_k[l], cache_diff_v[l], page_table, lam, rel_bias, diff_subln_w[l],
                             lam_init)
    o_m_s = mem_attn_sample(zs3, cache_mem_k[l], cache_mem_v[l])
    y_sample = _token_tail(xs, zs, o_r_s, o_d_s.reshape(db * ds, DIFF_V_W), o_m_s.reshape(db * ds, MEM_W),
                           wts).reshape(db, ds, d)

    nh2 = 2 * DIFF_HEADS
    return (
        y_prompt,
        y_sample,
        st_p[None],
        zp[:, OFF_DK:OFF_DK + DIFF_QK_W].reshape(1, b, s, nh2, DIFF_HD),
        zp[:, OFF_DV:OFF_DV + DIFF_V_W].reshape(1, b, s, DIFF_HEADS, 2 * DIFF_HD),
        mkv[:, :MEM_W].reshape(1, b, n_mem, MEM_HEADS, MEM_HD),
        mkv[:, MEM_W:].reshape(1, b, n_mem, MEM_HEADS, MEM_HD),
        st_s[None],
        zs[:, OFF_DK:OFF_DK + DIFF_QK_W].reshape(1, db, ds, nh2, DIFF_HD),
        zs[:, OFF_DV:OFF_DV + DIFF_V_W].reshape(1, db, ds, DIFF_HEADS, 2 * DIFF_HD),
    )
```

```python
import functools
import math

import numpy as np
import jax
import jax.numpy as jnp
from jax import lax
from jax.experimental import pallas as pl
from jax.experimental.pallas import tpu as pltpu

f32 = jnp.float32
bf16 = jnp.bfloat16

D_MODEL = 2048
RET_HEADS = 8
RET_DK = 128
RET_DV = 256
RET_CHUNK = 128
ROPE_BASE = 10000.0
DIFF_HEADS = 8
DIFF_HD = 64
REL_BUCKETS = 32
REL_MAX_DIST = 128
MEM_HEADS = 4
MEM_HD = 128
PEER_HEADS = 8
PEER_NKEYS = 128
PEER_TOPK = 16
PEER_DK = 128
EPS = 1e-6
NEG = -1e30

RET_QK_W = RET_HEADS * RET_DK
RET_V_W = RET_HEADS * RET_DV
DIFF_QK_W = 2 * DIFF_HEADS * DIFF_HD
DIFF_V_W = DIFF_HEADS * 2 * DIFF_HD
MEM_W = MEM_HEADS * MEM_HD
OFF_RQ = 0
OFF_RK = OFF_RQ + RET_QK_W
OFF_RV = OFF_RK + RET_QK_W
OFF_RG = OFF_RV + RET_V_W
OFF_DQ = OFF_RG + RET_V_W
OFF_DK = OFF_DQ + DIFF_QK_W
OFF_DV = OFF_DK + DIFF_QK_W
OFF_MQ = OFF_DV + DIFF_V_W
OFF_GA = OFF_MQ + MEM_W
OFF_GB = OFF_GA + D_MODEL
OFF_GC = OFF_GB + D_MODEL
IN_WIDTH = OFF_GC + D_MODEL

VMEM_LIMIT_BYTES = 56 * 1024 * 1024

NT = (((1,), (1,)), ((), ()))
TN = (((0,), (0,)), ((), ()))


def _cp(*sem):
    return pltpu.CompilerParams(dimension_semantics=sem, vmem_limit_bytes=VMEM_LIMIT_BYTES)


def _dot(a, b):
    return jnp.dot(a, b, preferred_element_type=f32)


def _dot_nt(a, b):
    return lax.dot_general(a, b, NT, preferred_element_type=f32)


def _dot_tn(a, b):
    return lax.dot_general(a, b, TN, preferred_element_type=f32)


def _rms_rows(x, g):
    y = x * lax.rsqrt(jnp.mean(x * x, axis=-1, keepdims=True) + EPS)
    return y * g


def _norm_matmul_kernel(x_ref, g_ref, w_ref, o_ref, *rest, rows):
    xn_ref = rest[-1]
    xn_out_ref = rest[0] if len(rest) == 2 else None

    @pl.when(pl.program_id(1) == 0)
    def _():
        def body(r, c):
            sl = pl.ds(pl.multiple_of(r * rows, rows), rows)
            xn = _rms_rows(x_ref[sl, :], g_ref[...]).astype(bf16)
            xn_ref[sl, :] = xn
            if xn_out_ref is not None:
                xn_out_ref[sl, :] = xn
            return c
        lax.fori_loop(0, x_ref.shape[0] // rows, body, 0)

    o_ref[...] = _dot(xn_ref[...], w_ref[...])


def norm_matmul(x, g, w, *, tm=1024, tn=512, with_rows=False):
    t, d = x.shape
    n = w.shape[1]
    tm = min(tm, t)
    assert t % tm == 0 and n % tn == 0 and tm % 16 == 0
    rows = 32 if tm % 32 == 0 else 16
    out_specs = [pl.BlockSpec((tm, tn), lambda i, j: (i, j))]
    out_shape = [jax.ShapeDtypeStruct((t, n), f32)]
    if with_rows:
        out_specs.append(pl.BlockSpec((tm, d), lambda i, j: (i, 0)))
        out_shape.append(jax.ShapeDtypeStruct((t, d), bf16))
    out = pl.pallas_call(
        functools.partial(_norm_matmul_kernel, rows=rows),
        grid=(t // tm, n // tn),
        in_specs=[
            pl.BlockSpec((tm, d), lambda i, j: (i, 0), pipeline_mode=pl.Buffered(1)),
            pl.BlockSpec((1, d), lambda i, j: (0, 0)),
            pl.BlockSpec((d, tn), lambda i, j: (0, j)),
        ],
        out_specs=out_specs,
        out_shape=out_shape,
        scratch_shapes=[pltpu.VMEM((tm, d), bf16)],
        compiler_params=_cp("parallel", "arbitrary"),
        name="norm_matmul",
    )(x, g.reshape(1, d), w)
    return out if with_rows else out[0]


def _ret_log_gamma():
    return jnp.log1p(-jnp.exp2(-5.0 - jnp.arange(RET_HEADS, dtype=f32)))


def _rope_tables(pos):
    half = RET_DK // 2
    inv = 1.0 / (ROPE_BASE ** jnp.linspace(0.0, 1.0, half, dtype=f32))
    ang = pos.astype(f32)[:, None] * inv[None, :]
    c, s = jnp.cos(ang), jnp.sin(ang)
    return jnp.concatenate([c, c], axis=-1), jnp.concatenate([-s, s], axis=-1)


def _rope(x, cos2, sin2):
    return x * cos2 + pltpu.roll(x, RET_DK // 2, 1) * sin2


def _ret_out(o, g):
    on = o * lax.rsqrt(jnp.mean(o * o, axis=-1, keepdims=True) + EPS)
    return (on * (g * jax.nn.sigmoid(g))).astype(bf16)


def _ret_prompt_kernel(decay_ref, q_ref, k_ref, v_ref, g_ref, cos_ref, sin_ref, dmask_ref, cs_ref, kdec_ref,
                       o_ref, st_out_ref, st_ref):
    c = pl.program_id(1)

    @pl.when(c == 0)
    def _():
        st_ref[...] = jnp.zeros_like(st_ref)

    cos2 = cos_ref[...]
    sin2 = sin_ref[...]
    for h in range(RET_HEADS):
        ks = slice(h * RET_DK, (h + 1) * RET_DK)
        vs = slice(h * RET_DV, (h + 1) * RET_DV)
        q = _rope(q_ref[:, ks], cos2, sin2)
        k = _rope(k_ref[:, ks], cos2, sin2) * (RET_DK ** -0.5)
        v = v_ref[:, vs]
        st = st_ref[h]
        qk = _dot_nt(q, k) * dmask_ref[h]
        o = _dot(qk, v) + _dot(q, st) * cs_ref[h]
        st_ref[h] = decay_ref[h] * st + _dot_tn(k * kdec_ref[h], v)
        o_ref[:, vs] = _ret_out(o, g_ref[:, vs])

    @pl.when(c == pl.num_programs(1) - 1)
    def _():
        st_out_ref[...] = st_ref[...]


def retention_prompt(z, batch, seq):
    cl = RET_CHUNK
    assert seq % cl == 0
    nc = seq // cl
    lg = _ret_log_gamma()[:, None]
    idx = jnp.arange(cl, dtype=f32)
    rel = idx[:, None] - idx[None, :]
    dmask = jnp.where(rel >= 0, jnp.exp(lg[:, :, None] * jnp.maximum(rel, 0.0)), 0.0)
    cs = jnp.broadcast_to(jnp.exp(lg * (idx + 1.0))[:, :, None], (RET_HEADS, cl, RET_DV))
    kdec = jnp.broadcast_to(jnp.exp(lg * (cl - 1.0 - idx))[:, :, None], (RET_HEADS, cl, RET_DK))
    decay = jnp.exp(lg[:, 0] * cl)
    cos2, sin2 = _rope_tables(jnp.arange(seq, dtype=jnp.int32))
    row = lambda b, c: b * nc + c
    const3 = lambda b, c: (0, 0, 0)
    return pl.pallas_call(
        _ret_prompt_kernel,
        grid=(batch, nc),
        in_specs=[
            pl.BlockSpec(memory_space=pltpu.SMEM),
            pl.BlockSpec((cl, RET_QK_W), lambda b, c: (row(b, c), OFF_RQ // RET_QK_W)),
            pl.BlockSpec((cl, RET_QK_W), lambda b, c: (row(b, c), OFF_RK // RET_QK_W)),
            pl.BlockSpec((cl, RET_V_W), lambda b, c: (row(b, c), OFF_RV // RET_V_W)),
            pl.BlockSpec((cl, RET_V_W), lambda b, c: (row(b, c), OFF_RG // RET_V_W)),
            pl.BlockSpec((cl, RET_DK), lambda b, c: (c, 0)),
            pl.BlockSpec((cl, RET_DK), lambda b, c: (c, 0)),
            pl.BlockSpec((RET_HEADS, cl, cl), const3),
            pl.BlockSpec((RET_HEADS, cl, RET_DV), const3),
            pl.BlockSpec((RET_HEADS, cl, RET_DK), const3),
        ],
        out_specs=[
            pl.BlockSpec((cl, RET_V_W), lambda b, c: (row(b, c), 0)),
            pl.BlockSpec((None, RET_HEADS, RET_DK, RET_DV), lambda b, c: (b, 0, 0, 0)),
        ],
        out_shape=[
            jax.ShapeDtypeStruct((batch * seq, RET_V_W), bf16),
            jax.ShapeDtypeStruct((batch, RET_HEADS, RET_DK, RET_DV), f32),
        ],
        scratch_shapes=[pltpu.VMEM((RET_HEADS, RET_DK, RET_DV), f32)],
        compiler_params=_cp("parallel", "arbitrary"),
        name="retention_prompt",
    )(decay, z, z, z, z, cos2, sin2, dmask, cs, kdec)


RET_SAMPLE_GROUP = 4


def _ret_sample_kernel(decay_ref, q_ref, k_ref, v_ref, g_ref, st_in_ref, cos_ref, sin_ref, dmask_ref, cs_ref,
                       kdec_ref, o_ref, st_out_ref, *, ds):
    cos2 = cos_ref[...]
    sin2 = sin_ref[...]
    rows = q_ref.shape[0]
    seq_of_row = lax.broadcasted_iota(jnp.int32, (rows, RET_DV), 0) // ds
    for h in range(RET_HEADS):
        ks = slice(h * RET_DK, (h + 1) * RET_DK)
        vs = slice(h * RET_DV, (h + 1) * RET_DV)
        q = _rope(q_ref[:, ks], cos2, sin2)
        k = _rope(k_ref[:, ks], cos2, sin2) * (RET_DK ** -0.5)
        v = v_ref[:, vs]
        kd = k * kdec_ref[h]
        qk = _dot_nt(q, k) * dmask_ref[h]
        cross = jnp.zeros((rows, RET_DV), f32)
        for s in range(RET_SAMPLE_GROUP):
            st = st_in_ref[s, h]
            cross = jnp.where(seq_of_row == s, _dot(q, st), cross)
            vb = jnp.where(seq_of_row == s, v, 0.0)
            st_out_ref[s, h] = decay_ref[h] * st + _dot_tn(kd, vb)
        o = _dot(qk, v) + cross * cs_ref[h]
        o_ref[:, vs] = _ret_out(o, g_ref[:, vs])


def retention_sample(z, state0, n_seq, ds, past):
    g = RET_SAMPLE_GROUP
    assert n_seq % g == 0 and (g * ds) % 8 == 0 and (ds % RET_CHUNK != 0)
    rows = g * ds
    lg = _ret_log_gamma()[:, None]
    idx = jnp.arange(ds, dtype=f32)
    rel = idx[:, None] - idx[None, :]
    dm = jnp.where(rel >= 0, jnp.exp(lg[:, :, None] * jnp.maximum(rel, 0.0)), 0.0)
    same_seq = jnp.kron(jnp.eye(g, dtype=f32), jnp.ones((ds, ds), f32))
    dmask = jnp.tile(dm, (1, g, g)) * same_seq[None]
    cs = jnp.broadcast_to(jnp.tile(jnp.exp(lg * (idx + 1.0)), (1, g))[:, :, None], (RET_HEADS, rows, RET_DV))
    kdec = jnp.broadcast_to(jnp.tile(jnp.exp(lg * (ds - 1.0 - idx)), (1, g))[:, :, None], (RET_HEADS, rows, RET_DK))
    decay = jnp.exp(lg[:, 0] * ds)
    cos2, sin2 = _rope_tables(past + jnp.arange(ds, dtype=jnp.int32))
    cos2, sin2 = jnp.tile(cos2, (g, 1)), jnp.tile(sin2, (g, 1))
    const2 = lambda i: (0, 0)
    const3 = lambda i: (0, 0, 0)
    st_spec = pl.BlockSpec((g, RET_HEADS, RET_DK, RET_DV), lambda i: (i, 0, 0, 0))
    return pl.pallas_call(
        functools.partial(_ret_sample_kernel, ds=ds),
        grid=(n_seq // g,),
        in_specs=[
            pl.BlockSpec(memory_space=pltpu.SMEM),
            pl.BlockSpec((rows, RET_QK_W), lambda i: (i, OFF_RQ // RET_QK_W)),
            pl.BlockSpec((rows, RET_QK_W), lambda i: (i, OFF_RK // RET_QK_W)),
            pl.BlockSpec((rows, RET_V_W), lambda i: (i, OFF_RV // RET_V_W)),
            pl.BlockSpec((rows, RET_V_W), lambda i: (i, OFF_RG // RET_V_W)),
            st_spec,
            pl.BlockSpec((rows, RET_DK), const2),
            pl.BlockSpec((rows, RET_DK), const2),
            pl.BlockSpec((RET_HEADS, rows, rows), const3),
            pl.BlockSpec((RET_HEADS, rows, RET_DV), const3),
            pl.BlockSpec((RET_HEADS, rows, RET_DK), const3),
        ],
        out_specs=[pl.BlockSpec((rows, RET_V_W), lambda i: (i, 0)), st_spec],
        out_shape=[
            jax.ShapeDtypeStruct((n_seq * ds, RET_V_W), bf16),
            jax.ShapeDtypeStruct((n_seq, RET_HEADS, RET_DK, RET_DV), f32),
        ],
        compiler_params=_cp("parallel"),
        name="retention_sample",
    )(decay, z, z, z, z, state0, cos2, sin2, dmask, cs, kdec)


def _lambda_init(layer):
    return 0.8 - 0.6 * math.exp(-0.3 * layer)


def _lambda_kernel(q1_ref, k1_ref, q2_ref, k2_ref, o_ref, *, lam_init):
    a = jnp.sum(q1_ref[...] * k1_ref[...], axis=-1, keepdims=True)
    b = jnp.sum(q2_ref[...] * k2_ref[...], axis=-1, keepdims=True)
    o_ref[...] = jnp.broadcast_to(jnp.exp(a) - jnp.exp(b) + lam_init, o_ref.shape)


def diff_lambda(q1, k1, q2, k2, lam_init):
    r = lambda a: a.reshape(1, DIFF_HD)
    return pl.pallas_call(
        functools.partial(_lambda_kernel, lam_init=lam_init),
        out_shape=jax.ShapeDtypeStruct((1, 2 * DIFF_HD), f32),
        name="diff_lambda",
    )(r(q1), r(k1), r(q2), r(k2))


def _rel_bucket_np(rel):
    n = np.maximum(rel, 0)
    max_exact = REL_BUCKETS // 2
    nf = np.maximum(n, 1).astype(np.float32)
    large = max_exact + (np.log(nf / np.float32(max_exact)) / np.float32(math.log(REL_MAX_DIST / max_exact))
                         * np.float32(REL_BUCKETS - max_exact)).astype(np.int32)
    large = np.minimum(large, REL_BUCKETS - 1)
    return np.where(n < max_exact, n, large).astype(np.int32)


def _bias_expand_kernel(rb_ref, tbl_ref, o_ref):
    h = pl.program_id(0)
    tbl = tbl_ref[...]
    acc = jnp.full(tbl.shape, NEG, f32)
    for bkt in range(REL_BUCKETS):
        acc = jnp.where(tbl == bkt, rb_ref[bkt, h], acc)
    o_ref[...] = acc


def bias_expand(bucket_tbl, rel_bias):
    r, c = bucket_tbl.shape
    return pl.pallas_call(
        _bias_expand_kernel,
        grid=(DIFF_HEADS,),
        in_specs=[pl.BlockSpec(memory_space=pltpu.SMEM), pl.BlockSpec((r, c), lambda h: (0, 0))],
        out_specs=pl.BlockSpec((None, r, c), lambda h: (h, 0, 0)),
        out_shape=jax.ShapeDtypeStruct((DIFF_HEADS, r, c), f32),
        compiler_params=_cp("parallel"),
        name="bias_expand",
    )(rel_bias.astype(f32), bucket_tbl)


def _subln_out(o, w, lam_init):
    on = o * lax.rsqrt(jnp.mean(o * o, axis=-1, keepdims=True) + EPS)
    return (on * w * (1.0 - lam_init)).astype(bf16)


def _softmax_step(s, v, m_old, l_old, acc_old):
    m_new = jnp.maximum(m_old, jnp.max(s, axis=-1, keepdims=True))
    p = jnp.exp(s - m_new)
    alpha = jnp.exp(m_old - m_new)
    return m_new, alpha * l_old + jnp.sum(p, axis=-1, keepdims=True), alpha * acc_old + _dot(p, v)


DIFF_TQ = 256
DIFF_TK = 512
DIFF_HP = 2


def _diff_prompt_kernel(lam_ref, q_ref, k_ref, v_ref, bias_ref, w_ref, o_ref, vt_ref, m_ref, l_ref, acc_ref, *,
                        lam_init):
    qi = pl.program_id(2)
    tq, tk, w2 = DIFF_TQ, DIFF_TK, 2 * DIFF_HD
    heads = [slice(hh * w2, (hh + 1) * w2) for hh in range(DIFF_HP)]

    @pl.when(qi == 0)
    def _():
        for hh, hs in enumerate(heads):
            vt_ref[hh] = v_ref[:, hs].T.astype(bf16)

    lane = lax.broadcasted_iota(jnp.int32, (tq, w2), 1)
    qm = []
    for hs in heads:
        q = q_ref[:, hs] * (DIFF_HD ** -0.5)
        qm.append((jnp.where(lane < DIFF_HD, q, 0.0).astype(bf16), jnp.where(lane >= DIFF_HD, q, 0.0).astype(bf16)))
    m_ref[...] = jnp.full(m_ref.shape, NEG, f32)
    l_ref[...] = jnp.zeros_like(l_ref)
    acc_ref[...] = jnp.zeros_like(acc_ref)

    def body(kb, carry):
        sl = pl.ds(pl.multiple_of(kb * tk, tk), tk)
        dist = jnp.minimum(qi - (tk // tq) * kb, bias_ref.shape[1] - 1)
        chains = [(hh, mp) for hh in range(DIFF_HP) for mp in range(2)]
        kblk = [k_ref[sl, hs].astype(bf16) for hs in heads]
        s = [_dot_nt(kblk[hh], qm[hh][mp]) + bias_ref[hh, dist] for hh, mp in chains]
        m_old = [m_ref[hh, mp] for hh, mp in chains]
        m_new = [jnp.maximum(mo, jnp.max(sc, axis=0, keepdims=True)) for mo, sc in zip(m_old, s)]
        p = [jnp.exp(sc - mn) for sc, mn in zip(s, m_new)]
        alpha = [jnp.exp(mo - mn) for mo, mn in zip(m_old, m_new)]
        for c, (hh, mp) in enumerate(chains):
            l_ref[hh, mp] = alpha[c] * l_ref[hh, mp] + jnp.sum(p[c], axis=0, keepdims=True)
            m_ref[hh, mp] = m_new[c]
        pv = [_dot(vt_ref[hh, :, sl], p[c].astype(bf16)) for c, (hh, mp) in enumerate(chains)]
        for c, (hh, mp) in enumerate(chains):
            acc_ref[hh, mp] = alpha[c] * acc_ref[hh, mp] + pv[c]
        return carry

    lax.fori_loop(0, qi // (tk // tq) + 1, body, 0)
    for hh, hs in enumerate(heads):
        o0 = (acc_ref[hh, 0] / l_ref[hh, 0]).T
        o1 = (acc_ref[hh, 1] / l_ref[hh, 1]).T
        o_ref[:, hs] = _subln_out(o0 - lam_ref[...] * o1, w_ref[...], lam_init)


def diff_attn_prompt(z, lam, rel_bias, subln_w, batch, seq, lam_init):
    tq, tk, hp = DIFF_TQ, DIFF_TK, DIFF_HP
    assert seq % tk == 0 and tk % tq == 0 and DIFF_HEADS % hp == 0
    nq = seq // tq
    kk = np.arange(tk)[:, None]
    qq = np.arange(tq)[None, :]
    tbs = []
    dist = 0
    while True:
        rel = dist * tq + qq - kk
        tb = np.where(rel >= 0, _rel_bucket_np(rel), -1)
        if rel.min() >= 0 and (tb == tb[0, 0]).all() and (_rel_bucket_np(np.arange(rel.min(), seq + 1)) == tb[0, 0]).all():
            tbs.append(tb)
            break
        tbs.append(tb)
        dist += 1
    nd = len(tbs)
    tbl = jnp.asarray(np.concatenate(tbs, axis=0).astype(np.int32))
    bias = bias_expand(tbl, rel_bias).reshape(DIFF_HEADS, nd, tk, tq)
    w2 = 2 * DIFF_HD
    wide = hp * w2
    return pl.pallas_call(
        functools.partial(_diff_prompt_kernel, lam_init=lam_init),
        grid=(batch, DIFF_HEADS // hp, nq),
        in_specs=[
            pl.BlockSpec((1, w2), lambda b, h, i: (0, 0)),
            pl.BlockSpec((tq, wide), lambda b, h, i: (b * nq + i, OFF_DQ // wide + h)),
            pl.BlockSpec((seq, wide), lambda b, h, i: (b, OFF_DK // wide + h)),
            pl.BlockSpec((seq, wide), lambda b, h, i: (b, OFF_DV // wide + h)),
            pl.BlockSpec((hp, nd, tk, tq), lambda b, h, i: (h, 0, 0, 0)),
            pl.BlockSpec((1, w2), lambda b, h, i: (0, 0)),
        ],
        out_specs=pl.BlockSpec((tq, wide), lambda b, h, i: (b * nq + i, h)),
        out_shape=jax.ShapeDtypeStruct((batch * seq, DIFF_V_W), bf16),
        scratch_shapes=[pltpu.VMEM((hp, w2, seq), bf16), pltpu.VMEM((hp, 2, 1, tq), f32),
                        pltpu.VMEM((hp, 2, 1, tq), f32), pltpu.VMEM((hp, 2, w2, tq), f32)],
        compiler_params=_cp("parallel", "parallel", "arbitrary"),
        name="diff_attn_prompt",
    )(lam, z, z, z, bias, subln_w.reshape(1, w2).astype(f32))


DIFF_SAMPLE_PAGES = 4


def _diff_sample_kernel(pt_ref, lam_ref, q_ref, kn_ref, vn_ref, *rest, ds, pg, lam_init):
    k_refs, v_refs = rest[:pg], rest[pg:2 * pg]
    bias_ref, biasn_ref, w_ref, o_ref, qbd_ref, kpad_ref, vpad_ref, m_ref, l_ref, acc_ref = rest[2 * pg:]
    b = pl.program_id(0)
    p = pl.program_id(1)
    w2 = 2 * DIFF_HD
    hr = 2 * ds

    @pl.when((b == 0) & (p == 0))
    def _():
        kpad_ref[...] = jnp.zeros_like(kpad_ref)
        vpad_ref[...] = jnp.zeros_like(vpad_ref)

    @pl.when(p == 0)
    def _():
        lane = lax.broadcasted_iota(jnp.int32, (ds, w2), 1)
        for h in range(DIFF_HEADS):
            qh = q_ref[:, h * w2:(h + 1) * w2] * (DIFF_HD ** -0.5)
            qbd_ref[h * hr:h * hr + ds, :] = jnp.where(lane < DIFF_HD, qh, 0.0)
            qbd_ref[h * hr + ds:(h + 1) * hr, :] = jnp.where(lane >= DIFF_HD, qh, 0.0)
        kpad_ref[0:ds, :] = kn_ref[...]
        vpad_ref[0:ds, :] = vn_ref[...]
        m_ref[...] = jnp.full(m_ref.shape, NEG, f32)
        l_ref[...] = jnp.zeros_like(l_ref)
        acc_ref[...] = jnp.zeros_like(acc_ref)

    head_rows = [slice(h * hr, (h + 1) * hr) for h in range(DIFF_HEADS)]

    def attend(scores, values, bias):
        s = jnp.concatenate([scores(h) for h in range(DIFF_HEADS)], axis=0) + bias
        m_old = m_ref[...]
        m_new = jnp.maximum(m_old, jnp.max(s, axis=-1, keepdims=True))
        p = jnp.exp(s - m_new)
        alpha = jnp.exp(m_old - m_new)
        l_ref[...] = alpha * l_ref[...] + jnp.sum(p, axis=-1, keepdims=True)
        m_ref[...] = m_new
        pv = jnp.concatenate([_dot(p[head_rows[h], :], values(h)) for h in range(DIFF_HEADS)], axis=0)
        acc_ref[...] = alpha * acc_ref[...] + pv

    page = k_refs[0].shape[-1]

    def page_scores(h):
        kt = jnp.concatenate([k_refs[i][2 * h:2 * h + 2].reshape(w2, page) for i in range(pg)], axis=1)
        return _dot(qbd_ref[head_rows[h], :], kt)

    def page_values(h):
        return jnp.concatenate([v_refs[i][pl.ds(h, page, stride=DIFF_HEADS), :] for i in range(pg)], axis=0)

    attend(page_scores, page_values, bias_ref[p])

    @pl.when(p == pl.num_programs(1) - 1)
    def _():
        attend(lambda h: _dot_nt(qbd_ref[head_rows[h], :], kpad_ref[:, h * w2:(h + 1) * w2]),
               lambda h: vpad_ref[:, h * w2:(h + 1) * w2], biasn_ref[...])
        for h in range(DIFF_HEADS):
            hs = slice(h * w2, (h + 1) * w2)
            rs = head_rows[h]
            blk = acc_ref[rs, :] / l_ref[rs, :]
            o = blk[0:ds] - lam_ref[...] * blk[ds:hr]
            o_ref[:, hs] = _subln_out(o, w_ref[...], lam_init)


def diff_attn_sample(z3, cache_k, cache_v, page_table, lam, rel_bias, subln_w, lam_init):
    n_seq, ds, _ = z3.shape
    n_pages = page_table.shape[1]
    page = cache_k.shape[-1]
    past = n_pages * page
    rows = 2 * DIFF_HEADS * ds
    pg = math.gcd(n_pages, DIFF_SAMPLE_PAGES)
    n_steps = n_pages // pg
    assert page == 128 and 2 * ds == 8
    qi = np.arange(ds)
    rel = past + qi[None, :, None] - np.arange(past).reshape(n_steps, 1, pg * page)
    tb = np.broadcast_to(_rel_bucket_np(rel)[:, None], (n_steps, 2, ds, pg * page))
    rel_new = qi[:, None] - qi[None, :]
    tbn = np.full((2, ds, page), -1, np.int32)
    tbn[:, :, :ds] = np.where(rel_new >= 0, _rel_bucket_np(rel_new), -1)[None]
    bias = bias_expand(jnp.asarray(tb.reshape(n_steps * 2 * ds, pg * page)), rel_bias)
    bias = bias.reshape(DIFF_HEADS, n_steps, 2 * ds, pg * page).transpose(1, 0, 2, 3).reshape(n_steps, rows, pg * page)
    biasn = bias_expand(jnp.asarray(tbn.reshape(2 * ds, page)), rel_bias).reshape(rows, page)
    w2 = 2 * DIFF_HD
    page_spec = lambda shape, i: pl.BlockSpec(
        (None,) + shape, lambda b, p, pt: (pt[b * n_pages + p * pg + i],) + (0,) * len(shape))
    grid_spec = pltpu.PrefetchScalarGridSpec(
        num_scalar_prefetch=1,
        grid=(n_seq, n_steps),
        in_specs=[
            pl.BlockSpec((1, w2), lambda b, p, pt: (0, 0)),
            pl.BlockSpec((None, ds, DIFF_QK_W), lambda b, p, pt: (b, 0, OFF_DQ // DIFF_QK_W)),
            pl.BlockSpec((None, ds, DIFF_QK_W), lambda b, p, pt: (b, 0, OFF_DK // DIFF_QK_W)),
            pl.BlockSpec((None, ds, DIFF_V_W), lambda b, p, pt: (b, 0, OFF_DV // DIFF_V_W)),
            *[page_spec(cache_k.shape[1:], i) for i in range(pg)],
            *[page_spec(cache_v.shape[1:], i) for i in range(pg)],
            pl.BlockSpec((n_steps, rows, pg * page), lambda b, p, pt: (0, 0, 0)),
            pl.BlockSpec((rows, page), lambda b, p, pt: (0, 0)),
            pl.BlockSpec((1, w2), lambda b, p, pt: (0, 0)),
        ],
        out_specs=pl.BlockSpec((None, ds, DIFF_V_W), lambda b, p, pt: (b, 0, 0)),
        scratch_shapes=[
            pltpu.VMEM((rows, w2), f32),
            pltpu.VMEM((page, DIFF_QK_W), f32),
            pltpu.VMEM((page, DIFF_V_W), f32),
            pltpu.VMEM((rows, 1), f32),
            pltpu.VMEM((rows, 1), f32),
            pltpu.VMEM((rows, w2), f32),
        ],
    )
    return pl.pallas_call(
        functools.partial(_diff_sample_kernel, ds=ds, pg=pg, lam_init=lam_init),
        grid_spec=grid_spec,
        out_shape=jax.ShapeDtypeStruct((n_seq, ds, DIFF_V_W), bf16),
        compiler_params=_cp("arbitrary", "arbitrary"),
        name="diff_attn_sample",
    )(page_table.reshape(-1), lam, z3, z3, z3, *([cache_k] * pg), *([cache_v] * pg), bias, biasn,
      subln_w.reshape(1, w2).astype(f32))


def _mem_attn_head(q, mk, mv):
    s = _dot_nt(q, mk) * (MEM_HD ** -0.5)
    m = jnp.max(s, axis=-1, keepdims=True)
    p = jnp.exp(s - m)
    return (_dot(p, mv) / jnp.sum(p, axis=-1, keepdims=True)).astype(bf16)


def _mem_attn_kernel(q_ref, mk_ref, mv_ref, o_ref):
    for h in range(MEM_HEADS):
        hs = slice(h * MEM_HD, (h + 1) * MEM_HD)
        o_ref[:, hs] = _mem_attn_head(q_ref[:, hs], mk_ref[:, hs], mv_ref[:, hs])


def _mem_attn_sample_kernel(q_ref, mk_ref, mv_ref, o_ref):
    n_mem = mk_ref.shape[0] // MEM_HEADS
    for h in range(MEM_HEADS):
        hs = slice(h * MEM_HD, (h + 1) * MEM_HD)
        rows = pl.ds(h, n_mem, stride=MEM_HEADS)
        o_ref[:, hs] = _mem_attn_head(q_ref[:, hs], mk_ref[rows, :], mv_ref[rows, :])


def mem_attn_prompt(z, mkv, batch, seq, n_mem, *, tq=256):
    assert seq % tq == 0
    nq = seq // tq
    return pl.pallas_call(
        _mem_attn_kernel,
        grid=(batch, nq),
        in_specs=[
            pl.BlockSpec((tq, MEM_W), lambda b, i: (b * nq + i, OFF_MQ // MEM_W)),
            pl.BlockSpec((n_mem, MEM_W), lambda b, i: (b, 0)),
            pl.BlockSpec((n_mem, MEM_W), lambda b, i: (b, 1)),
        ],
        out_specs=pl.BlockSpec((tq, MEM_W), lambda b, i: (b * nq + i, 0)),
        out_shape=jax.ShapeDtypeStruct((batch * seq, MEM_W), bf16),
        compiler_params=_cp("parallel", "parallel"),
        name="mem_attn_prompt",
    )(z, mkv, mkv)


def mem_attn_sample(z3, mk, mv):
    n_seq, ds, _ = z3.shape
    mem_spec = pl.BlockSpec((None, mk.shape[1], MEM_HD), lambda b: (b, 0, 0))
    return pl.pallas_call(
        _mem_attn_sample_kernel,
        grid=(n_seq,),
        in_specs=[pl.BlockSpec((None, ds, MEM_W), lambda b: (b, 0, OFF_MQ // MEM_W)), mem_spec, mem_spec],
        out_specs=pl.BlockSpec((None, ds, MEM_W), lambda b: (b, 0, 0)),
        out_shape=jax.ShapeDtypeStruct((n_seq, ds, MEM_W), bf16),
        compiler_params=_cp("parallel"),
        name="mem_attn_sample",
    )(z3, mk, mv)


def _mix_kernel(orr_ref, od_ref, om_ref, ga_ref, gb_ref, gc_ref, wr_ref, wd_ref, wm_ref, o_ref):
    mixed = (jax.nn.sigmoid(ga_ref[...]) * _dot(orr_ref[...], wr_ref[...])
             + jax.nn.sigmoid(gb_ref[...]) * _dot(od_ref[...], wd_ref[...])
             + jax.nn.sigmoid(gc_ref[...]) * _dot(om_ref[...], wm_ref[...]))
    o_ref[...] = mixed.astype(bf16)


def gated_mix(o_r, o_d, o_m, z, w_r, w_d, w_m, *, tm=1024, tn=512):
    t = o_r.shape[0]
    tm = min(tm, t)
    assert t % tm == 0 and D_MODEL % tn == 0
    gate = lambda off: pl.BlockSpec((tm, tn), lambda i, j: (i, off // tn + j))
    act = lambda w: pl.BlockSpec((tm, w), lambda i, j: (i, 0))
    wgt = lambda w: pl.BlockSpec((w, tn), lambda i, j: (0, j))
    return pl.pallas_call(
        _mix_kernel,
        grid=(t // tm, D_MODEL // tn),
        in_specs=[act(RET_V_W), act(DIFF_V_W), act(MEM_W), gate(OFF_GA), gate(OFF_GB), gate(OFF_GC),
                  wgt(RET_V_W), wgt(DIFF_V_W), wgt(MEM_W)],
        out_specs=pl.BlockSpec((tm, tn), lambda i, j: (i, j)),
        out_shape=jax.ShapeDtypeStruct((t, D_MODEL), bf16),
        compiler_params=_cp("parallel", "parallel"),
        name="gated_mix",
    )(o_r, o_d, o_m, z, z, z, w_r, w_d, w_m)


def _matmul_residual_kernel(x_ref, a_ref, w_ref, o_ref):
    o_ref[...] = x_ref[...] + _dot(a_ref[...], w_ref[...])


def matmul_residual(x, a, w, *, tm=1024, tn=512):
    t, k = a.shape
    n = w.shape[1]
    tm = min(tm, t)
    assert t % tm == 0 and n % tn == 0
    return pl.pallas_call(
        _matmul_residual_kernel,
        grid=(t // tm, n // tn),
        in_specs=[
            pl.BlockSpec((tm, tn), lambda i, j: (i, j)),
            pl.BlockSpec((tm, k), lambda i, j: (i, 0)),
            pl.BlockSpec((k, tn), lambda i, j: (0, j)),
        ],
        out_specs=pl.BlockSpec((tm, tn), lambda i, j: (i, j)),
        out_shape=jax.ShapeDtypeStruct((t, n), f32),
        compiler_params=_cp("parallel", "parallel"),
        name="matmul_residual",
    )(x, a, w)


_PEER_CAND = [(a, b) for a in range(PEER_TOPK) for b in range(PEER_TOPK) if (a + 1) * (b + 1) <= PEER_TOPK]


def _top_values(x, k):
    out = []
    for i in range(k):
        mx = jnp.max(x, axis=0, keepdims=True)
        out.append(mx)
        if i + 1 < k:
            x = jnp.where(x == mx, -jnp.inf, x)
    return out


def _peer_select_kernel(pq_ref, keys_ref, a1_ref, a2_ref, e1_ref, e2_ref, tau_ref):
    for h in range(PEER_HEADS):
        sc = []
        for c in range(2):
            off = (2 * h + c) * PEER_DK
            sc.append(_dot_nt(keys_ref[h, c], pq_ref[:, off:off + PEER_DK]))
        a1_ref[h] = sc[0]
        a2_ref[h] = sc[1]
        t1 = _top_values(sc[0], PEER_TOPK)
        t2 = _top_values(sc[1], PEER_TOPK)
        cand = jnp.concatenate([t1[a] + t2[b] for a, b in _PEER_CAND], axis=0)
        top = _top_values(cand, PEER_TOPK)
        m = top[0]
        zsum = jnp.ones_like(m)
        for r in range(1, PEER_TOPK):
            zsum = zsum + jnp.exp(top[r] - m)
        tau_ref[h:h + 1, :] = top[PEER_TOPK - 1]
        e1_ref[h] = jnp.exp(sc[0] - t1[0]) / zsum
        e2_ref[h] = jnp.exp(sc[1] - t2[0])


def peer_select(pq, keys, *, tt=256):
    t = pq.shape[0]
    tt = min(tt, t)
    assert t % tt == 0
    big = pl.BlockSpec((PEER_HEADS, PEER_NKEYS, tt), lambda i: (0, 0, i))
    shp = jax.ShapeDtypeStruct((PEER_HEADS, PEER_NKEYS, t), f32)
    return pl.pallas_call(
        _peer_select_kernel,
        grid=(t // tt,),
        in_specs=[
            pl.BlockSpec((tt, PEER_HEADS * 2 * PEER_DK), lambda i: (i, 0)),
            pl.BlockSpec((PEER_HEADS, 2, PEER_NKEYS, PEER_DK), lambda i: (0, 0, 0, 0)),
        ],
        out_specs=[big, big, big, big, pl.BlockSpec((PEER_HEADS, tt), lambda i: (0, i))],
        out_shape=[shp, shp, shp, shp, jax.ShapeDtypeStruct((PEER_HEADS, t), f32)],
        compiler_params=_cp("parallel"),
        name="peer_select",
    )(pq, keys)


PEER_ROWS = 8


def _gelu(x):
    return 0.5 * x * (1.0 + lax.erf(x * (2.0 ** -0.5)))


PEER_SUB = 16


def _peer_weights_block(c0, i0, ht_ref, wt_ref, a1_ref, e1_ref, a2_ref, e2_ref, tau_ref):
    cs = slice(c0, c0 + 128)
    isl = slice(i0, i0 + PEER_SUB)
    g = [None] * PEER_ROWS
    for h in range(PEER_HEADS):
        a2 = a2_ref[h, isl, cs]
        e2 = e2_ref[h, isl, cs]
        tau = tau_ref[h:h + 1, cs]
        for r in range(PEER_ROWS):
            sc = a1_ref[h, r:r + 1, cs] + a2
            w = e1_ref[h, r:r + 1, cs] * e2
            sel = jnp.where(sc >= tau, w, 0.0)
            g[r] = sel if g[r] is None else g[r] + sel
    for r in range(PEER_ROWS):
        rs = slice(r * PEER_NKEYS + i0, r * PEER_NKEYS + i0 + PEER_SUB)
        wt_ref[rs, cs] = (g[r] * _gelu(ht_ref[rs, cs])).astype(bf16)


def _spread(n_items, n_slots):
    return [range(-(-s * n_items // n_slots), -(-(s + 1) * n_items // n_slots)) for s in range(n_slots)]


def _peer_dense_kernel(xn_ref, u_ref, vt_ref, a1_ref, e1_ref, a2_ref, e2_ref, tau_ref, o_ref,
                       ht0_ref, ht1_ref, wt0_ref, wt1_ref, acc_ref, *, nj):
    s = pl.program_id(0)
    j_out = lax.rem(s + (nj - 2), nj)

    @pl.when(s == 0)
    def _():
        for ref in (ht0_ref, ht1_ref, wt0_ref, wt1_ref, acc_ref):
            ref[...] = jnp.zeros_like(ref)

    @pl.when((s >= 2) & (j_out == 0))
    def _():
        acc_ref[...] = jnp.zeros_like(acc_ref)

    ec, tt = ht0_ref.shape
    d = acc_ref.shape[0]
    mt = 256
    nt = min(256, tt)

    def step(ht_w, ht_r, wt_w, wt_r):
        def mm_scores(m0, n0):
            ht_w[m0:m0 + mt, n0:n0 + nt] = _dot_nt(u_ref[m0:m0 + mt, :], xn_ref[n0:n0 + nt, :])

        def mm_out(m0, n0):
            acc_ref[m0:m0 + mt, n0:n0 + nt] += _dot(vt_ref[m0:m0 + mt, :], wt_r[:, n0:n0 + nt])

        pieces = [(mm_scores, m0, n0) for m0 in range(0, ec, mt) for n0 in range(0, tt, nt)]
        pieces += [(mm_out, m0, n0) for m0 in range(0, d, mt) for n0 in range(0, tt, nt)]
        n1 = (ec // mt) * (tt // nt)
        pieces = [p for _, _, p in sorted(
            ((k + 0.5) / n1 if k < n1 else (k - n1 + 0.5) / (len(pieces) - n1), k, p)
            for k, p in enumerate(pieces))]
        blocks = [(c0, i0) for c0 in range(0, tt, 128) for i0 in range(0, PEER_NKEYS, PEER_SUB)]
        for (c0, i0), todo in zip(blocks, _spread(len(pieces), len(blocks))):
            for k in todo:
                fn, m0, n0 = pieces[k]
                fn(m0, n0)
            _peer_weights_block(c0, i0, ht_r, wt_w, a1_ref, e1_ref, a2_ref, e2_ref, tau_ref)

    even = lax.rem(s, 2) == 0

    @pl.when(even)
    def _():
        step(ht0_ref, ht1_ref, wt1_ref, wt0_ref)

    @pl.when(jnp.logical_not(even))
    def _():
        step(ht1_ref, ht0_ref, wt0_ref, wt1_ref)

    @pl.when((s >= 2) & (j_out == nj - 1))
    def _():
        o_ref[...] = acc_ref[...].T


def peer_dense(xn, u, vt, a1, a2, e1, e2, tau, *, tt=512):
    t, d = xn.shape
    n_exp = u.shape[0]
    tt = min(tt, t)
    ec = PEER_ROWS * PEER_NKEYS
    assert t % tt == 0 and n_exp % ec == 0 and tt % 128 == 0
    nj = n_exp // ec
    n = (t // tt) * nj
    assert nj > 2
    c_mm1 = lambda s: jnp.minimum(s, n - 1)
    c_wts = lambda s: jnp.clip(s - 1, 0, n - 1)
    c_mm2 = lambda s: jnp.clip(s - 2, 0, n - 1)
    once = pl.Buffered(1)
    row_blk = pl.BlockSpec((PEER_HEADS, PEER_ROWS, tt), lambda s: (0, c_wts(s) % nj, c_wts(s) // nj))
    all_blk = pl.BlockSpec((PEER_HEADS, PEER_NKEYS, tt), lambda s: (0, 0, c_wts(s) // nj), pipeline_mode=once)
    return pl.pallas_call(
        functools.partial(_peer_dense_kernel, nj=nj),
        grid=(n + 2,),
        in_specs=[
            pl.BlockSpec((tt, d), lambda s: (c_mm1(s) // nj, 0), pipeline_mode=once),
            pl.BlockSpec((ec, d), lambda s: (c_mm1(s) % nj, 0)),
            pl.BlockSpec((d, ec), lambda s: (0, c_mm2(s) % nj)),
            row_blk, row_blk, all_blk, all_blk,
            pl.BlockSpec((PEER_HEADS, tt), lambda s: (0, c_wts(s) // nj)),
        ],
        out_specs=pl.BlockSpec((tt, d), lambda s: (c_mm2(s) // nj, 0)),
        out_shape=jax.ShapeDtypeStruct((t, d), f32),
        scratch_shapes=[pltpu.VMEM((ec, tt), f32), pltpu.VMEM((ec, tt), f32), pltpu.VMEM((ec, tt), bf16),
                        pltpu.VMEM((ec, tt), bf16), pltpu.VMEM((d, tt), f32)],
        compiler_params=_cp("arbitrary"),
        name="peer_dense",
    )(xn, u, vt, a1, e1, a2, e2, tau)


def _final_norm_kernel(h_ref, p_ref, g_ref, o_ref):
    o_ref[...] = _rms_rows(h_ref[...] + p_ref[...], g_ref[...])


def final_norm(h, p, g, *, tm=256):
    t, d = h.shape
    tm = min(tm, t)
    assert t % tm == 0
    blk = pl.BlockSpec((tm, d), lambda i: (i, 0))
    return pl.pallas_call(
        _final_norm_kernel,
        grid=(t // tm,),
        in_specs=[blk, blk, pl.BlockSpec((1, d), lambda i: (0, 0))],
        out_specs=blk,
        out_shape=jax.ShapeDtypeStruct((t, d), f32),
        compiler_params=_cp("parallel"),
        name="final_norm",
    )(h, p, g.reshape(1, d))


def _token_tail(x2, z, o_r, o_d, o_m, wts):
    mixed = gated_mix(o_r, o_d, o_m, z, wts["w_up_ret"], wts["w_up_diff"], wts["w_up_mem"])
    h = matmul_residual(x2, mixed, wts["w_out"])
    pq, xn = norm_matmul(h, wts["norm_ffn_w"], wts["peer_w_q"], with_rows=True)
    a1, a2, e1, e2, tau = peer_select(pq, wts["peer_keys"])
    peer = peer_dense(xn, wts["peer_u"], wts["peer_vt"], a1, a2, e1, e2, tau)
    return final_norm(h, peer, wts["norm_final_w"])


def kernel(x_prompt, x_sample, mem_prompt, state_ret, cache_diff_k, cache_diff_v, cache_mem_k, cache_mem_v, page_table, norm_mix_w, norm_mem_w, norm_ffn_w, norm_final_w, w_in, w_mem_k, w_mem_v, diff_lambda_q1, diff_lambda_k1, diff_lambda_q2, diff_lambda_k2, diff_subln_w, rel_bias, w_up_ret, w_up_diff, w_up_mem, w_out, peer_w_q, peer_keys, peer_u, peer_v):
    b, s, d = x_prompt.shape
    db, ds, _ = x_sample.shape
    n_pages = page_table.shape[1]
    page = cache_diff_k.shape[2]
    past = n_pages * page
    n_mem = mem_prompt.shape[1]
    assert w_in.shape[0] == 1 and d == D_MODEL, "single-layer trunk"
    l = 0
    lam_init = _lambda_init(l)

    wts = {
        "w_up_ret": w_up_ret[l].astype(bf16), "w_up_diff": w_up_diff[l].astype(bf16),
        "w_up_mem": w_up_mem[l].astype(bf16), "w_out": w_out[l].astype(bf16),
        "norm_ffn_w": norm_ffn_w[l], "peer_w_q": peer_w_q[l].astype(bf16), "peer_keys": peer_keys[l],
        "peer_u": peer_u[l].astype(bf16), "peer_vt": peer_v[l].astype(bf16).T, "norm_final_w": norm_final_w,
    }
    w_in_b = w_in[l].astype(bf16)
    lam = diff_lambda(diff_lambda_q1[l], diff_lambda_k1[l], diff_lambda_q2[l], diff_lambda_k2[l], lam_init)

    xp = x_prompt.reshape(b * s, d)
    zp = norm_matmul(xp, norm_mix_w[l], w_in_b)
    w_mem = jnp.concatenate([w_mem_k[l], w_mem_v[l]], axis=1).astype(bf16)
    mkv = norm_matmul(mem_prompt.reshape(b * n_mem, d), norm_mem_w[l], w_mem)
    o_r, st_p = retention_prompt(zp, b, s)
    o_d = diff_attn_prompt(zp, lam, rel_bias, diff_subln_w[l], b, s, lam_init)
    o_m = mem_attn_prompt(zp, mkv, b, s, n_mem)
    y_prompt = _token_tail(xp, zp, o_r, o_d, o_m, wts).reshape(b, s, d)

    xs = x_sample.reshape(db * ds, d)
    zs = norm_matmul(xs, norm_mix_w[l], w_in_b)
    zs3 = zs.reshape(db, ds, IN_WIDTH)
    o_r_s, st_s = retention_sample(zs, state_ret[l], db, ds, past)
    ck = jnp.transpose(cache_diff_k[l], (0, 2, 3, 1))
    cv = cache_diff_v[l].reshape(-1, page * DIFF_HEADS, 2 * DIFF_HD)
    o_d_s = diff_attn_sample(zs3, ck, cv, page_table, lam, rel_bias, diff_subln_w[l], lam_init)
    o_m_s = mem_attn_sample(zs3, cache_mem_k[l].reshape(db, n_mem * MEM_HEADS, MEM_HD),
                            cache_mem_v[l].reshape(db, n_mem * MEM_HEADS, MEM_HD))
    y_sample = _token_tail(xs, zs, o_r_s, o_d_s.reshape(db * ds, DIFF_V_W), o_m_s.reshape(db * ds, MEM_W),
                           wts).reshape(db, ds, d)

    nh2 = 2 * DIFF_HEADS
    return (
        y_prompt,
        y_sample,
        st_p[None],
        zp[:, OFF_DK:OFF_DK + DIFF_QK_W].reshape(1, b, s, nh2, DIFF_HD),
        zp[:, OFF_DV:OFF_DV + DIFF_V_W].reshape(1, b, s, DIFF_HEADS, 2 * DIFF_HD),
        mkv[:, :MEM_W].reshape(1, b, n_mem, MEM_HEADS, MEM_HD),
        mkv[:, MEM_W:].reshape(1, b, n_mem, MEM_HEADS, MEM_HD),
        st_s[None],
        zs[:, OFF_DK:OFF_DK + DIFF_QK_W].reshape(1, db, ds, nh2, DIFF_HD),
        zs[:, OFF_DV:OFF_DV + DIFF_V_W].reshape(1, db, ds, DIFF_HEADS, 2 * DIFF_HD),
    )
```

```python
import functools
import math

import numpy as np
import jax
import jax.numpy as jnp
from jax import lax
from jax.experimental import pallas as pl
from jax.experimental.pallas import tpu as pltpu

f32 = jnp.float32
bf16 = jnp.bfloat16

D_MODEL = 2048
RET_HEADS = 8
RET_DK = 128
RET_DV = 256
RET_CHUNK = 128
ROPE_BASE = 10000.0
DIFF_HEADS = 8
DIFF_HD = 64
REL_BUCKETS = 32
REL_MAX_DIST = 128
MEM_HEADS = 4
MEM_HD = 128
PEER_HEADS = 8
PEER_NKEYS = 128
PEER_TOPK = 16
PEER_DK = 128
EPS = 1e-6
NEG = -1e30

RET_QK_W = RET_HEADS * RET_DK
RET_V_W = RET_HEADS * RET_DV
DIFF_QK_W = 2 * DIFF_HEADS * DIFF_HD
DIFF_V_W = DIFF_HEADS * 2 * DIFF_HD
MEM_W = MEM_HEADS * MEM_HD
OFF_RQ = 0
OFF_RK = OFF_RQ + RET_QK_W
OFF_RV = OFF_RK + RET_QK_W
OFF_RG = OFF_RV + RET_V_W
OFF_DQ = OFF_RG + RET_V_W
OFF_DK = OFF_DQ + DIFF_QK_W
OFF_DV = OFF_DK + DIFF_QK_W
OFF_MQ = OFF_DV + DIFF_V_W
OFF_GA = OFF_MQ + MEM_W
OFF_GB = OFF_GA + D_MODEL
OFF_GC = OFF_GB + D_MODEL
IN_WIDTH = OFF_GC + D_MODEL

VMEM_LIMIT_BYTES = 56 * 1024 * 1024

NT = (((1,), (1,)), ((), ()))
TN = (((0,), (0,)), ((), ()))


def _cp(*sem, flags=None):
    return pltpu.CompilerParams(dimension_semantics=sem, vmem_limit_bytes=VMEM_LIMIT_BYTES, flags=flags)


def _dot(a, b):
    return jnp.dot(a, b, preferred_element_type=f32)


def _dot_nt(a, b):
    return lax.dot_general(a, b, NT, preferred_element_type=f32)


def _dot_tn(a, b):
    return lax.dot_general(a, b, TN, preferred_element_type=f32)


def _rms_rows(x, g):
    y = x * lax.rsqrt(jnp.mean(x * x, axis=-1, keepdims=True) + EPS)
    return y * g


def _norm_matmul_kernel(x_ref, g_ref, w_ref, o_ref, *rest, rows):
    xn_ref = rest[-1]
    xn_out_ref = rest[0] if len(rest) == 2 else None

    @pl.when(pl.program_id(1) == 0)
    def _():
        def body(r, c):
            sl = pl.ds(pl.multiple_of(r * rows, rows), rows)
            xn = _rms_rows(x_ref[sl, :], g_ref[...]).astype(bf16)
            xn_ref[sl, :] = xn
            if xn_out_ref is not None:
                xn_out_ref[sl, :] = xn
            return c
        lax.fori_loop(0, x_ref.shape[0] // rows, body, 0)

    o_ref[...] = _dot(xn_ref[...], w_ref[...])


def norm_matmul(x, g, w, *, tm=1024, tn=512, with_rows=False):
    t, d = x.shape
    n = w.shape[1]
    tm = min(tm, t)
    assert t % tm == 0 and n % tn == 0 and tm % 16 == 0
    rows = 32 if tm % 32 == 0 else 16
    out_specs = [pl.BlockSpec((tm, tn), lambda i, j: (i, j))]
    out_shape = [jax.ShapeDtypeStruct((t, n), f32)]
    if with_rows:
        out_specs.append(pl.BlockSpec((tm, d), lambda i, j: (i, 0)))
        out_shape.append(jax.ShapeDtypeStruct((t, d), bf16))
    out = pl.pallas_call(
        functools.partial(_norm_matmul_kernel, rows=rows),
        grid=(t // tm, n // tn),
        in_specs=[
            pl.BlockSpec((tm, d), lambda i, j: (i, 0), pipeline_mode=pl.Buffered(1)),
            pl.BlockSpec((1, d), lambda i, j: (0, 0)),
            pl.BlockSpec((d, tn), lambda i, j: (0, j)),
        ],
        out_specs=out_specs,
        out_shape=out_shape,
        scratch_shapes=[pltpu.VMEM((tm, d), bf16)],
        compiler_params=_cp("parallel", "arbitrary"),
        name="norm_matmul",
    )(x, g.reshape(1, d), w)
    return out if with_rows else out[0]


def _ret_log_gamma():
    return jnp.log1p(-jnp.exp2(-5.0 - jnp.arange(RET_HEADS, dtype=f32)))


def _rope_tables(pos):
    half = RET_DK // 2
    inv = 1.0 / (ROPE_BASE ** jnp.linspace(0.0, 1.0, half, dtype=f32))
    ang = pos.astype(f32)[:, None] * inv[None, :]
    c, s = jnp.cos(ang), jnp.sin(ang)
    return jnp.concatenate([c, c], axis=-1), jnp.concatenate([-s, s], axis=-1)


def _rope(x, cos2, sin2):
    return x * cos2 + pltpu.roll(x, RET_DK // 2, 1) * sin2


def _ret_out(o, g):
    on = o * lax.rsqrt(jnp.mean(o * o, axis=-1, keepdims=True) + EPS)
    return (on * (g * jax.nn.sigmoid(g))).astype(bf16)


def _ret_prompt_kernel(decay_ref, q_ref, k_ref, v_ref, g_ref, cos_ref, sin_ref, dmask_ref, cs_ref, kdec_ref,
                       o_ref, st_out_ref, st_ref):
    c = pl.program_id(1)

    @pl.when(c == 0)
    def _():
        st_ref[...] = jnp.zeros_like(st_ref)

    cos2 = cos_ref[...]
    sin2 = sin_ref[...]
    for h in range(RET_HEADS):
        ks = slice(h * RET_DK, (h + 1) * RET_DK)
        vs = slice(h * RET_DV, (h + 1) * RET_DV)
        q = _rope(q_ref[:, ks], cos2, sin2)
        k = _rope(k_ref[:, ks], cos2, sin2) * (RET_DK ** -0.5)
        v = v_ref[:, vs]
        st = st_ref[h]
        qk = _dot_nt(q, k) * dmask_ref[h]
        o = _dot(qk, v) + _dot(q, st) * cs_ref[h]
        st_ref[h] = decay_ref[h] * st + _dot_tn(k * kdec_ref[h], v)
        o_ref[:, vs] = _ret_out(o, g_ref[:, vs])

    @pl.when(c == pl.num_programs(1) - 1)
    def _():
        st_out_ref[...] = st_ref[...]


def retention_prompt(z, batch, seq):
    cl = RET_CHUNK
    assert seq % cl == 0
    nc = seq // cl
    lg = _ret_log_gamma()[:, None]
    idx = jnp.arange(cl, dtype=f32)
    rel = idx[:, None] - idx[None, :]
    dmask = jnp.where(rel >= 0, jnp.exp(lg[:, :, None] * jnp.maximum(rel, 0.0)), 0.0)
    cs = jnp.broadcast_to(jnp.exp(lg * (idx + 1.0))[:, :, None], (RET_HEADS, cl, RET_DV))
    kdec = jnp.broadcast_to(jnp.exp(lg * (cl - 1.0 - idx))[:, :, None], (RET_HEADS, cl, RET_DK))
    decay = jnp.exp(lg[:, 0] * cl)
    cos2, sin2 = _rope_tables(jnp.arange(seq, dtype=jnp.int32))
    row = lambda b, c: b * nc + c
    const3 = lambda b, c: (0, 0, 0)
    return pl.pallas_call(
        _ret_prompt_kernel,
        grid=(batch, nc),
        in_specs=[
            pl.BlockSpec(memory_space=pltpu.SMEM),
            pl.BlockSpec((cl, RET_QK_W), lambda b, c: (row(b, c), OFF_RQ // RET_QK_W)),
            pl.BlockSpec((cl, RET_QK_W), lambda b, c: (row(b, c), OFF_RK // RET_QK_W)),
            pl.BlockSpec((cl, RET_V_W), lambda b, c: (row(b, c), OFF_RV // RET_V_W)),
            pl.BlockSpec((cl, RET_V_W), lambda b, c: (row(b, c), OFF_RG // RET_V_W)),
            pl.BlockSpec((cl, RET_DK), lambda b, c: (c, 0)),
            pl.BlockSpec((cl, RET_DK), lambda b, c: (c, 0)),
            pl.BlockSpec((RET_HEADS, cl, cl), const3),
            pl.BlockSpec((RET_HEADS, cl, RET_DV), const3),
            pl.BlockSpec((RET_HEADS, cl, RET_DK), const3),
        ],
        out_specs=[
            pl.BlockSpec((cl, RET_V_W), lambda b, c: (row(b, c), 0)),
            pl.BlockSpec((None, RET_HEADS, RET_DK, RET_DV), lambda b, c: (b, 0, 0, 0)),
        ],
        out_shape=[
            jax.ShapeDtypeStruct((batch * seq, RET_V_W), bf16),
            jax.ShapeDtypeStruct((batch, RET_HEADS, RET_DK, RET_DV), f32),
        ],
        scratch_shapes=[pltpu.VMEM((RET_HEADS, RET_DK, RET_DV), f32)],
        compiler_params=_cp("parallel", "arbitrary"),
        name="retention_prompt",
    )(decay, z, z, z, z, cos2, sin2, dmask, cs, kdec)


RET_SAMPLE_GROUP = 4


def _ret_sample_kernel(decay_ref, q_ref, k_ref, v_ref, g_ref, st_in_ref, cos_ref, sin_ref, dmask_ref, cs_ref,
                       kdec_ref, o_ref, st_out_ref, *, ds):
    cos2 = cos_ref[...]
    sin2 = sin_ref[...]
    rows = q_ref.shape[0]
    seq_of_row = lax.broadcasted_iota(jnp.int32, (rows, RET_DV), 0) // ds
    for h in range(RET_HEADS):
        ks = slice(h * RET_DK, (h + 1) * RET_DK)
        vs = slice(h * RET_DV, (h + 1) * RET_DV)
        q = _rope(q_ref[:, ks], cos2, sin2)
        k = _rope(k_ref[:, ks], cos2, sin2) * (RET_DK ** -0.5)
        v = v_ref[:, vs]
        kd = k * kdec_ref[h]
        qk = _dot_nt(q, k) * dmask_ref[h]
        cross = jnp.zeros((rows, RET_DV), f32)
        for s in range(RET_SAMPLE_GROUP):
            st = st_in_ref[s, h]
            cross = jnp.where(seq_of_row == s, _dot(q, st), cross)
            vb = jnp.where(seq_of_row == s, v, 0.0)
            st_out_ref[s, h] = decay_ref[h] * st + _dot_tn(kd, vb)
        o = _dot(qk, v) + cross * cs_ref[h]
        o_ref[:, vs] = _ret_out(o, g_ref[:, vs])


def retention_sample(z, state0, n_seq, ds, past):
    g = RET_SAMPLE_GROUP
    assert n_seq % g == 0 and (g * ds) % 8 == 0 and (ds % RET_CHUNK != 0)
    rows = g * ds
    lg = _ret_log_gamma()[:, None]
    idx = jnp.arange(ds, dtype=f32)
    rel = idx[:, None] - idx[None, :]
    dm = jnp.where(rel >= 0, jnp.exp(lg[:, :, None] * jnp.maximum(rel, 0.0)), 0.0)
    same_seq = jnp.kron(jnp.eye(g, dtype=f32), jnp.ones((ds, ds), f32))
    dmask = jnp.tile(dm, (1, g, g)) * same_seq[None]
    cs = jnp.broadcast_to(jnp.tile(jnp.exp(lg * (idx + 1.0)), (1, g))[:, :, None], (RET_HEADS, rows, RET_DV))
    kdec = jnp.broadcast_to(jnp.tile(jnp.exp(lg * (ds - 1.0 - idx)), (1, g))[:, :, None], (RET_HEADS, rows, RET_DK))
    decay = jnp.exp(lg[:, 0] * ds)
    cos2, sin2 = _rope_tables(past + jnp.arange(ds, dtype=jnp.int32))
    cos2, sin2 = jnp.tile(cos2, (g, 1)), jnp.tile(sin2, (g, 1))
    const2 = lambda i: (0, 0)
    const3 = lambda i: (0, 0, 0)
    st_spec = pl.BlockSpec((g, RET_HEADS, RET_DK, RET_DV), lambda i: (i, 0, 0, 0))
    return pl.pallas_call(
        functools.partial(_ret_sample_kernel, ds=ds),
        grid=(n_seq // g,),
        in_specs=[
            pl.BlockSpec(memory_space=pltpu.SMEM),
            pl.BlockSpec((rows, RET_QK_W), lambda i: (i, OFF_RQ // RET_QK_W)),
            pl.BlockSpec((rows, RET_QK_W), lambda i: (i, OFF_RK // RET_QK_W)),
            pl.BlockSpec((rows, RET_V_W), lambda i: (i, OFF_RV // RET_V_W)),
            pl.BlockSpec((rows, RET_V_W), lambda i: (i, OFF_RG // RET_V_W)),
            st_spec,
            pl.BlockSpec((rows, RET_DK), const2),
            pl.BlockSpec((rows, RET_DK), const2),
            pl.BlockSpec((RET_HEADS, rows, rows), const3),
            pl.BlockSpec((RET_HEADS, rows, RET_DV), const3),
            pl.BlockSpec((RET_HEADS, rows, RET_DK), const3),
        ],
        out_specs=[pl.BlockSpec((rows, RET_V_W), lambda i: (i, 0)), st_spec],
        out_shape=[
            jax.ShapeDtypeStruct((n_seq * ds, RET_V_W), bf16),
            jax.ShapeDtypeStruct((n_seq, RET_HEADS, RET_DK, RET_DV), f32),
        ],
        compiler_params=_cp("parallel"),
        name="retention_sample",
    )(decay, z, z, z, z, state0, cos2, sin2, dmask, cs, kdec)


def _lambda_init(layer):
    return 0.8 - 0.6 * math.exp(-0.3 * layer)


def _lambda_kernel(q1_ref, k1_ref, q2_ref, k2_ref, o_ref, *, lam_init):
    a = jnp.sum(q1_ref[...] * k1_ref[...], axis=-1, keepdims=True)
    b = jnp.sum(q2_ref[...] * k2_ref[...], axis=-1, keepdims=True)
    o_ref[...] = jnp.broadcast_to(jnp.exp(a) - jnp.exp(b) + lam_init, o_ref.shape)


def diff_lambda(q1, k1, q2, k2, lam_init):
    r = lambda a: a.reshape(1, DIFF_HD)
    return pl.pallas_call(
        functools.partial(_lambda_kernel, lam_init=lam_init),
        out_shape=jax.ShapeDtypeStruct((1, 2 * DIFF_HD), f32),
        name="diff_lambda",
    )(r(q1), r(k1), r(q2), r(k2))


def _rel_bucket_np(rel):
    n = np.maximum(rel, 0)
    max_exact = REL_BUCKETS // 2
    nf = np.maximum(n, 1).astype(np.float32)
    large = max_exact + (np.log(nf / np.float32(max_exact)) / np.float32(math.log(REL_MAX_DIST / max_exact))
                         * np.float32(REL_BUCKETS - max_exact)).astype(np.int32)
    large = np.minimum(large, REL_BUCKETS - 1)
    return np.where(n < max_exact, n, large).astype(np.int32)


def _bias_expand_kernel(rb_ref, tbl_ref, o_ref):
    h = pl.program_id(0)
    tbl = tbl_ref[...]
    acc = jnp.full(tbl.shape, NEG, f32)
    for bkt in range(REL_BUCKETS):
        acc = jnp.where(tbl == bkt, rb_ref[bkt, h], acc)
    o_ref[...] = acc


def bias_expand(bucket_tbl, rel_bias):
    r, c = bucket_tbl.shape
    return pl.pallas_call(
        _bias_expand_kernel,
        grid=(DIFF_HEADS,),
        in_specs=[pl.BlockSpec(memory_space=pltpu.SMEM), pl.BlockSpec((r, c), lambda h: (0, 0))],
        out_specs=pl.BlockSpec((None, r, c), lambda h: (h, 0, 0)),
        out_shape=jax.ShapeDtypeStruct((DIFF_HEADS, r, c), f32),
        compiler_params=_cp("parallel"),
        name="bias_expand",
    )(rel_bias.astype(f32), bucket_tbl)


def _subln_out(o, w, lam_init):
    on = o * lax.rsqrt(jnp.mean(o * o, axis=-1, keepdims=True) + EPS)
    return (on * w * (1.0 - lam_init)).astype(bf16)


def _softmax_step(s, v, m_old, l_old, acc_old):
    m_new = jnp.maximum(m_old, jnp.max(s, axis=-1, keepdims=True))
    p = jnp.exp(s - m_new)
    alpha = jnp.exp(m_old - m_new)
    return m_new, alpha * l_old + jnp.sum(p, axis=-1, keepdims=True), alpha * acc_old + _dot(p, v)


DIFF_TQ = 256
DIFF_TK = 512
DIFF_HP = 2


def _diff_prompt_kernel(lam_ref, q_ref, k_ref, v_ref, bias_ref, w_ref, o_ref, vt_ref, m_ref, l_ref, acc_ref, *,
                        lam_init):
    qi = pl.program_id(2)
    tq, tk, w2 = DIFF_TQ, DIFF_TK, 2 * DIFF_HD
    heads = [slice(hh * w2, (hh + 1) * w2) for hh in range(DIFF_HP)]

    @pl.when(qi == 0)
    def _():
        for hh, hs in enumerate(heads):
            vt_ref[hh] = v_ref[:, hs].T.astype(bf16)

    lane = lax.broadcasted_iota(jnp.int32, (tq, w2), 1)
    qm = []
    for hs in heads:
        q = q_ref[:, hs] * (DIFF_HD ** -0.5)
        qm.append((jnp.where(lane < DIFF_HD, q, 0.0).astype(bf16), jnp.where(lane >= DIFF_HD, q, 0.0).astype(bf16)))
    m_ref[...] = jnp.full(m_ref.shape, NEG, f32)
    l_ref[...] = jnp.zeros_like(l_ref)
    acc_ref[...] = jnp.zeros_like(acc_ref)

    def body(kb, carry):
        sl = pl.ds(pl.multiple_of(kb * tk, tk), tk)
        dist = jnp.minimum(qi - (tk // tq) * kb, bias_ref.shape[1] - 1)
        chains = [(hh, mp) for hh in range(DIFF_HP) for mp in range(2)]
        kblk = [k_ref[sl, hs].astype(bf16) for hs in heads]
        s = [_dot_nt(kblk[hh], qm[hh][mp]) + bias_ref[hh, dist] for hh, mp in chains]
        m_old = [m_ref[hh, mp] for hh, mp in chains]
        m_new = [jnp.maximum(mo, jnp.max(sc, axis=0, keepdims=True)) for mo, sc in zip(m_old, s)]
        p = [jnp.exp(sc - mn) for sc, mn in zip(s, m_new)]
        alpha = [jnp.exp(mo - mn) for mo, mn in zip(m_old, m_new)]
        for c, (hh, mp) in enumerate(chains):
            l_ref[hh, mp] = alpha[c] * l_ref[hh, mp] + jnp.sum(p[c], axis=0, keepdims=True)
            m_ref[hh, mp] = m_new[c]
        pv = [_dot(vt_ref[hh, :, sl], p[c].astype(bf16)) for c, (hh, mp) in enumerate(chains)]
        for c, (hh, mp) in enumerate(chains):
            acc_ref[hh, mp] = alpha[c] * acc_ref[hh, mp] + pv[c]
        return carry

    lax.fori_loop(0, qi // (tk // tq) + 1, body, 0)
    for hh, hs in enumerate(heads):
        o0 = (acc_ref[hh, 0] / l_ref[hh, 0]).T
        o1 = (acc_ref[hh, 1] / l_ref[hh, 1]).T
        o_ref[:, hs] = _subln_out(o0 - lam_ref[...] * o1, w_ref[...], lam_init)


def diff_attn_prompt(z, lam, rel_bias, subln_w, batch, seq, lam_init):
    tq, tk, hp = DIFF_TQ, DIFF_TK, DIFF_HP
    assert seq % tk == 0 and tk % tq == 0 and DIFF_HEADS % hp == 0
    nq = seq // tq
    kk = np.arange(tk)[:, None]
    qq = np.arange(tq)[None, :]
    tbs = []
    dist = 0
    while True:
        rel = dist * tq + qq - kk
        tb = np.where(rel >= 0, _rel_bucket_np(rel), -1)
        if rel.min() >= 0 and (tb == tb[0, 0]).all() and (_rel_bucket_np(np.arange(rel.min(), seq + 1)) == tb[0, 0]).all():
            tbs.append(tb)
            break
        tbs.append(tb)
        dist += 1
    nd = len(tbs)
    tbl = jnp.asarray(np.concatenate(tbs, axis=0).astype(np.int32))
    bias = bias_expand(tbl, rel_bias).reshape(DIFF_HEADS, nd, tk, tq)
    w2 = 2 * DIFF_HD
    wide = hp * w2
    return pl.pallas_call(
        functools.partial(_diff_prompt_kernel, lam_init=lam_init),
        grid=(batch, DIFF_HEADS // hp, nq),
        in_specs=[
            pl.BlockSpec((1, w2), lambda b, h, i: (0, 0)),
            pl.BlockSpec((tq, wide), lambda b, h, i: (b * nq + i, OFF_DQ // wide + h)),
            pl.BlockSpec((seq, wide), lambda b, h, i: (b, OFF_DK // wide + h)),
            pl.BlockSpec((seq, wide), lambda b, h, i: (b, OFF_DV // wide + h)),
            pl.BlockSpec((hp, nd, tk, tq), lambda b, h, i: (h, 0, 0, 0)),
            pl.BlockSpec((1, w2), lambda b, h, i: (0, 0)),
        ],
        out_specs=pl.BlockSpec((tq, wide), lambda b, h, i: (b * nq + i, h)),
        out_shape=jax.ShapeDtypeStruct((batch * seq, DIFF_V_W), bf16),
        scratch_shapes=[pltpu.VMEM((hp, w2, seq), bf16), pltpu.VMEM((hp, 2, 1, tq), f32),
                        pltpu.VMEM((hp, 2, 1, tq), f32), pltpu.VMEM((hp, 2, w2, tq), f32)],
        compiler_params=_cp("parallel", "parallel", "arbitrary"),
        name="diff_attn_prompt",
    )(lam, z, z, z, bias, subln_w.reshape(1, w2).astype(f32))


DIFF_SAMPLE_PAGES = 8


def _diff_sample_kernel(pt_ref, lam_ref, q_ref, kn_ref, vn_ref, *rest, ds, pg, lam_init):
    k_refs, v_refs = rest[:pg], rest[pg:2 * pg]
    bias_ref, biasn_ref, w_ref, o_ref, qbd_ref, kpad_ref, vpad_ref, m_ref, l_ref, acc_ref = rest[2 * pg:]
    b = pl.program_id(0)
    p = pl.program_id(1)
    w2 = 2 * DIFF_HD
    hr = 2 * ds

    @pl.when((b == 0) & (p == 0))
    def _():
        kpad_ref[...] = jnp.zeros_like(kpad_ref)
        vpad_ref[...] = jnp.zeros_like(vpad_ref)

    @pl.when(p == 0)
    def _():
        lane = lax.broadcasted_iota(jnp.int32, (ds, w2), 1)
        for h in range(DIFF_HEADS):
            qh = q_ref[:, h * w2:(h + 1) * w2] * (DIFF_HD ** -0.5)
            qbd_ref[h * hr:h * hr + ds, :] = jnp.where(lane < DIFF_HD, qh, 0.0)
            qbd_ref[h * hr + ds:(h + 1) * hr, :] = jnp.where(lane >= DIFF_HD, qh, 0.0)
        kpad_ref[0:ds, :] = kn_ref[...]
        vpad_ref[0:ds, :] = vn_ref[...]
        m_ref[...] = jnp.full(m_ref.shape, NEG, f32)
        l_ref[...] = jnp.zeros_like(l_ref)
        acc_ref[...] = jnp.zeros_like(acc_ref)

    head_rows = [slice(h * hr, (h + 1) * hr) for h in range(DIFF_HEADS)]

    def attend(scores, values, bias):
        s = jnp.concatenate([scores(h) for h in range(DIFF_HEADS)], axis=0) + bias
        m_old = m_ref[...]
        m_new = jnp.maximum(m_old, jnp.max(s, axis=-1, keepdims=True))
        p = jnp.exp(s - m_new)
        alpha = jnp.exp(m_old - m_new)
        l_ref[...] = alpha * l_ref[...] + jnp.sum(p, axis=-1, keepdims=True)
        m_ref[...] = m_new
        pv = jnp.concatenate([_dot(p[head_rows[h], :], values(h)) for h in range(DIFF_HEADS)], axis=0)
        acc_ref[...] = alpha * acc_ref[...] + pv

    page = k_refs[0].shape[-1]

    def page_scores(h):
        kt = jnp.concatenate([k_refs[i][2 * h:2 * h + 2].reshape(w2, page) for i in range(pg)], axis=1)
        return _dot(qbd_ref[head_rows[h], :], kt)

    def page_values(h):
        return jnp.concatenate([v_refs[i][pl.ds(h, page, stride=DIFF_HEADS), :] for i in range(pg)], axis=0)

    attend(page_scores, page_values, bias_ref[p])

    @pl.when(p == pl.num_programs(1) - 1)
    def _():
        attend(lambda h: _dot_nt(qbd_ref[head_rows[h], :], kpad_ref[:, h * w2:(h + 1) * w2]),
               lambda h: vpad_ref[:, h * w2:(h + 1) * w2], biasn_ref[...])
        for h in range(DIFF_HEADS):
            hs = slice(h * w2, (h + 1) * w2)
            rs = head_rows[h]
            blk = acc_ref[rs, :] / l_ref[rs, :]
            o = blk[0:ds] - lam_ref[...] * blk[ds:hr]
            o_ref[:, hs] = _subln_out(o, w_ref[...], lam_init)


def diff_attn_sample(z3, cache_k, cache_v, page_table, lam, rel_bias, subln_w, lam_init):
    n_seq, ds, _ = z3.shape
    n_pages = page_table.shape[1]
    page = cache_k.shape[-1]
    past = n_pages * page
    rows = 2 * DIFF_HEADS * ds
    pg = math.gcd(n_pages, DIFF_SAMPLE_PAGES)
    n_steps = n_pages // pg
    assert page == 128 and 2 * ds == 8
    qi = np.arange(ds)
    rel = past + qi[None, :, None] - np.arange(past).reshape(n_steps, 1, pg * page)
    tb = np.broadcast_to(_rel_bucket_np(rel)[:, None], (n_steps, 2, ds, pg * page))
    rel_new = qi[:, None] - qi[None, :]
    tbn = np.full((2, ds, page), -1, np.int32)
    tbn[:, :, :ds] = np.where(rel_new >= 0, _rel_bucket_np(rel_new), -1)[None]
    bias = bias_expand(jnp.asarray(tb.reshape(n_steps * 2 * ds, pg * page)), rel_bias)
    bias = bias.reshape(DIFF_HEADS, n_steps, 2 * ds, pg * page).transpose(1, 0, 2, 3).reshape(n_steps, rows, pg * page)
    biasn = bias_expand(jnp.asarray(tbn.reshape(2 * ds, page)), rel_bias).reshape(rows, page)
    w2 = 2 * DIFF_HD
    page_spec = lambda shape, i: pl.BlockSpec(
        (None,) + shape, lambda b, p, pt: (pt[b * n_pages + p * pg + i],) + (0,) * len(shape))
    grid_spec = pltpu.PrefetchScalarGridSpec(
        num_scalar_prefetch=1,
        grid=(n_seq, n_steps),
        in_specs=[
            pl.BlockSpec((1, w2), lambda b, p, pt: (0, 0)),
            pl.BlockSpec((None, ds, DIFF_QK_W), lambda b, p, pt: (b, 0, OFF_DQ // DIFF_QK_W)),
            pl.BlockSpec((None, ds, DIFF_QK_W), lambda b, p, pt: (b, 0, OFF_DK // DIFF_QK_W)),
            pl.BlockSpec((None, ds, DIFF_V_W), lambda b, p, pt: (b, 0, OFF_DV // DIFF_V_W)),
            *[page_spec(cache_k.shape[1:], i) for i in range(pg)],
            *[page_spec(cache_v.shape[1:], i) for i in range(pg)],
            pl.BlockSpec((n_steps, rows, pg * page), lambda b, p, pt: (0, 0, 0)),
            pl.BlockSpec((rows, page), lambda b, p, pt: (0, 0)),
            pl.BlockSpec((1, w2), lambda b, p, pt: (0, 0)),
        ],
        out_specs=pl.BlockSpec((None, ds, DIFF_V_W), lambda b, p, pt: (b, 0, 0)),
        scratch_shapes=[
            pltpu.VMEM((rows, w2), f32),
            pltpu.VMEM((page, DIFF_QK_W), f32),
            pltpu.VMEM((page, DIFF_V_W), f32),
            pltpu.VMEM((rows, 1), f32),
            pltpu.VMEM((rows, 1), f32),
            pltpu.VMEM((rows, w2), f32),
        ],
    )
    return pl.pallas_call(
        functools.partial(_diff_sample_kernel, ds=ds, pg=pg, lam_init=lam_init),
        grid_spec=grid_spec,
        out_shape=jax.ShapeDtypeStruct((n_seq, ds, DIFF_V_W), bf16),
        compiler_params=_cp("arbitrary", "arbitrary"),
        name="diff_attn_sample",
    )(page_table.reshape(-1), lam, z3, z3, z3, *([cache_k] * pg), *([cache_v] * pg), bias, biasn,
      subln_w.reshape(1, w2).astype(f32))


def _mem_attn_head(q, mk, mv):
    s = _dot_nt(q, mk) * (MEM_HD ** -0.5)
    m = jnp.max(s, axis=-1, keepdims=True)
    p = jnp.exp(s - m)
    return (_dot(p, mv) / jnp.sum(p, axis=-1, keepdims=True)).astype(bf16)


def _mem_attn_kernel(q_ref, mk_ref, mv_ref, o_ref):
    for h in range(MEM_HEADS):
        hs = slice(h * MEM_HD, (h + 1) * MEM_HD)
        o_ref[:, hs] = _mem_attn_head(q_ref[:, hs], mk_ref[:, hs], mv_ref[:, hs])


def _mem_attn_sample_kernel(q_ref, mk_ref, mv_ref, o_ref):
    n_mem = mk_ref.shape[0] // MEM_HEADS
    for h in range(MEM_HEADS):
        hs = slice(h * MEM_HD, (h + 1) * MEM_HD)
        rows = pl.ds(h, n_mem, stride=MEM_HEADS)
        o_ref[:, hs] = _mem_attn_head(q_ref[:, hs], mk_ref[rows, :], mv_ref[rows, :])


def mem_attn_prompt(z, mkv, batch, seq, n_mem, *, tq=256):
    assert seq % tq == 0
    nq = seq // tq
    return pl.pallas_call(
        _mem_attn_kernel,
        grid=(batch, nq),
        in_specs=[
            pl.BlockSpec((tq, MEM_W), lambda b, i: (b * nq + i, OFF_MQ // MEM_W)),
            pl.BlockSpec((n_mem, MEM_W), lambda b, i: (b, 0)),
            pl.BlockSpec((n_mem, MEM_W), lambda b, i: (b, 1)),
        ],
        out_specs=pl.BlockSpec((tq, MEM_W), lambda b, i: (b * nq + i, 0)),
        out_shape=jax.ShapeDtypeStruct((batch * seq, MEM_W), bf16),
        compiler_params=_cp("parallel", "parallel"),
        name="mem_attn_prompt",
    )(z, mkv, mkv)


def mem_attn_sample(z3, mk, mv):
    n_seq, ds, _ = z3.shape
    mem_spec = pl.BlockSpec((None, mk.shape[1], MEM_HD), lambda b: (b, 0, 0))
    return pl.pallas_call(
        _mem_attn_sample_kernel,
        grid=(n_seq,),
        in_specs=[pl.BlockSpec((None, ds, MEM_W), lambda b: (b, 0, OFF_MQ // MEM_W)), mem_spec, mem_spec],
        out_specs=pl.BlockSpec((None, ds, MEM_W), lambda b: (b, 0, 0)),
        out_shape=jax.ShapeDtypeStruct((n_seq, ds, MEM_W), bf16),
        compiler_params=_cp("parallel"),
        name="mem_attn_sample",
    )(z3, mk, mv)


def _mix_kernel(orr_ref, od_ref, om_ref, ga_ref, gb_ref, gc_ref, wr_ref, wd_ref, wm_ref, o_ref):
    mixed = (jax.nn.sigmoid(ga_ref[...]) * _dot(orr_ref[...], wr_ref[...])
             + jax.nn.sigmoid(gb_ref[...]) * _dot(od_ref[...], wd_ref[...])
             + jax.nn.sigmoid(gc_ref[...]) * _dot(om_ref[...], wm_ref[...]))
    o_ref[...] = mixed.astype(bf16)


def gated_mix(o_r, o_d, o_m, z, w_r, w_d, w_m, *, tm=1024, tn=512):
    t = o_r.shape[0]
    tm = min(tm, t)
    assert t % tm == 0 and D_MODEL % tn == 0
    gate = lambda off: pl.BlockSpec((tm, tn), lambda i, j: (i, off // tn + j))
    act = lambda w: pl.BlockSpec((tm, w), lambda i, j: (i, 0))
    wgt = lambda w: pl.BlockSpec((w, tn), lambda i, j: (0, j))
    return pl.pallas_call(
        _mix_kernel,
        grid=(t // tm, D_MODEL // tn),
        in_specs=[act(RET_V_W), act(DIFF_V_W), act(MEM_W), gate(OFF_GA), gate(OFF_GB), gate(OFF_GC),
                  wgt(RET_V_W), wgt(DIFF_V_W), wgt(MEM_W)],
        out_specs=pl.BlockSpec((tm, tn), lambda i, j: (i, j)),
        out_shape=jax.ShapeDtypeStruct((t, D_MODEL), bf16),
        compiler_params=_cp("parallel", "parallel"),
        name="gated_mix",
    )(o_r, o_d, o_m, z, z, z, w_r, w_d, w_m)


def _matmul_residual_kernel(x_ref, a_ref, w_ref, o_ref):
    o_ref[...] = x_ref[...] + _dot(a_ref[...], w_ref[...])


def matmul_residual(x, a, w, *, tm=1024, tn=512):
    t, k = a.shape
    n = w.shape[1]
    tm = min(tm, t)
    assert t % tm == 0 and n % tn == 0
    return pl.pallas_call(
        _matmul_residual_kernel,
        grid=(t // tm, n // tn),
        in_specs=[
            pl.BlockSpec((tm, tn), lambda i, j: (i, j)),
            pl.BlockSpec((tm, k), lambda i, j: (i, 0)),
            pl.BlockSpec((k, tn), lambda i, j: (0, j)),
        ],
        out_specs=pl.BlockSpec((tm, tn), lambda i, j: (i, j)),
        out_shape=jax.ShapeDtypeStruct((t, n), f32),
        compiler_params=_cp("parallel", "parallel"),
        name="matmul_residual",
    )(x, a, w)


_PEER_CAND = [(a, b) for a in range(PEER_TOPK) for b in range(PEER_TOPK) if (a + 1) * (b + 1) <= PEER_TOPK]


def _top_values(x, k):
    out = []
    for i in range(k):
        mx = jnp.max(x, axis=0, keepdims=True)
        out.append(mx)
        if i + 1 < k:
            x = jnp.where(x == mx, -jnp.inf, x)
    return out


def _peer_select_kernel(pq_ref, keys_ref, a1_ref, a2_ref, e1_ref, e2_ref, tau_ref):
    for h in range(PEER_HEADS):
        sc = []
        for c in range(2):
            off = (2 * h + c) * PEER_DK
            sc.append(_dot_nt(keys_ref[h, c], pq_ref[:, off:off + PEER_DK]))
        a1_ref[h] = sc[0]
        a2_ref[h] = sc[1]
        t1 = _top_values(sc[0], PEER_TOPK)
        t2 = _top_values(sc[1], PEER_TOPK)
        cand = jnp.concatenate([t1[a] + t2[b] for a, b in _PEER_CAND], axis=0)
        top = _top_values(cand, PEER_TOPK)
        m = top[0]
        zsum = jnp.ones_like(m)
        for r in range(1, PEER_TOPK):
            zsum = zsum + jnp.exp(top[r] - m)
        tau_ref[h:h + 1, :] = top[PEER_TOPK - 1]
        e1_ref[h] = jnp.exp(sc[0] - t1[0]) / zsum
        e2_ref[h] = jnp.exp(sc[1] - t2[0])


def peer_select(pq, keys, *, tt=256):
    t = pq.shape[0]
    tt = min(tt, t)
    assert t % tt == 0
    big = pl.BlockSpec((PEER_HEADS, PEER_NKEYS, tt), lambda i: (0, 0, i))
    shp = jax.ShapeDtypeStruct((PEER_HEADS, PEER_NKEYS, t), f32)
    return pl.pallas_call(
        _peer_select_kernel,
        grid=(t // tt,),
        in_specs=[
            pl.BlockSpec((tt, PEER_HEADS * 2 * PEER_DK), lambda i: (i, 0)),
            pl.BlockSpec((PEER_HEADS, 2, PEER_NKEYS, PEER_DK), lambda i: (0, 0, 0, 0)),
        ],
        out_specs=[big, big, big, big, pl.BlockSpec((PEER_HEADS, tt), lambda i: (0, i))],
        out_shape=[shp, shp, shp, shp, jax.ShapeDtypeStruct((PEER_HEADS, t), f32)],
        compiler_params=_cp("parallel"),
        name="peer_select",
    )(pq, keys)


PEER_ROWS = 8


def _gelu(x):
    return 0.5 * x * (1.0 + lax.erf(x * (2.0 ** -0.5)))


PEER_SUB = 16


def _peer_weights_block(c0, i0, ht_ref, wt_ref, a1_ref, e1_ref, a2_ref, e2_ref, tau_ref):
    cs = slice(c0, c0 + 128)
    isl = slice(i0, i0 + PEER_SUB)
    g = [None] * PEER_ROWS
    for h in range(PEER_HEADS):
        a2 = a2_ref[h, isl, cs]
        e2 = e2_ref[h, isl, cs]
        tau = tau_ref[h:h + 1, cs]
        for r in range(PEER_ROWS):
            sc = a1_ref[h, r:r + 1, cs] + a2
            w = e1_ref[h, r:r + 1, cs] * e2
            sel = jnp.where(sc >= tau, w, 0.0)
            g[r] = sel if g[r] is None else g[r] + sel
    out = []
    for r in range(PEER_ROWS):
        rs = slice(r * PEER_NKEYS + i0, r * PEER_NKEYS + i0 + PEER_SUB)
        out.append((rs, cs, (g[r] * _gelu(ht_ref[rs, cs])).astype(bf16)))

    def store():
        for rs, cs_, val in out:
            wt_ref[rs, cs_] = val
    return store


def _spread(n_items, n_slots):
    return [range(-(-s * n_items // n_slots), -(-(s + 1) * n_items // n_slots)) for s in range(n_slots)]


def _peer_dense_kernel(xnt_ref, u_ref, vt_ref, a1_ref, e1_ref, a2_ref, e2_ref, tau_ref, o_ref,
                       ht0_ref, ht1_ref, wt0_ref, wt1_ref, acc_ref, *, nj):
    s = pl.program_id(0)
    j_out = lax.rem(s + (nj - 2), nj)

    @pl.when(s == 0)
    def _():
        for ref in (ht0_ref, ht1_ref, wt0_ref, wt1_ref, acc_ref):
            ref[...] = jnp.zeros_like(ref)

    @pl.when((s >= 2) & (j_out == 0))
    def _():
        acc_ref[...] = jnp.zeros_like(acc_ref)

    ec, tt = ht0_ref.shape
    d = acc_ref.shape[0]
    mt = 256
    nt = min(256, tt)

    def step(ht_w, ht_r, wt_w, wt_r):
        def mm_scores(m0, n0):
            val = _dot(u_ref[m0:m0 + mt, :], xnt_ref[:, n0:n0 + nt])

            def store():
                ht_w[m0:m0 + mt, n0:n0 + nt] = val
            return store

        def mm_out(m0, n0):
            val = acc_ref[m0:m0 + mt, n0:n0 + nt] + _dot(vt_ref[m0:m0 + mt, :], wt_r[:, n0:n0 + nt])

            def store():
                acc_ref[m0:m0 + mt, n0:n0 + nt] = val
            return store

        pieces = [(mm_scores, m0, n0) for m0 in range(0, ec, mt) for n0 in range(0, tt, nt)]
        pieces += [(mm_out, m0, n0) for m0 in range(0, d, mt) for n0 in range(0, tt, nt)]
        n1 = (ec // mt) * (tt // nt)
        pieces = [p for _, _, p in sorted(
            ((k + 0.5) / n1 if k < n1 else (k - n1 + 0.5) / (len(pieces) - n1), k, p)
            for k, p in enumerate(pieces))]
        blocks = [(c0, i0) for c0 in range(0, tt, 128) for i0 in range(0, PEER_NKEYS, PEER_SUB)]
        for (c0, i0), todo in zip(blocks, _spread(len(pieces), len(blocks))):
            stores = [pieces[k][0](*pieces[k][1:]) for k in todo]
            stores.append(_peer_weights_block(c0, i0, ht_r, wt_w, a1_ref, e1_ref, a2_ref, e2_ref, tau_ref))
            for store in stores:
                store()

    even = lax.rem(s, 2) == 0

    @pl.when(even)
    def _():
        step(ht0_ref, ht1_ref, wt1_ref, wt0_ref)

    @pl.when(jnp.logical_not(even))
    def _():
        step(ht1_ref, ht0_ref, wt0_ref, wt1_ref)

    @pl.when((s >= 2) & (j_out == nj - 1))
    def _():
        o_ref[...] = acc_ref[...].T


def peer_dense(xnt, u, vt, a1, a2, e1, e2, tau, *, tt=512):
    d, t = xnt.shape
    n_exp = u.shape[0]
    tt = min(tt, t)
    ec = PEER_ROWS * PEER_NKEYS
    assert t % tt == 0 and n_exp % ec == 0 and tt % 128 == 0
    nj = n_exp // ec
    n = (t // tt) * nj
    assert nj > 2
    c_mm1 = lambda s: jnp.minimum(s, n - 1)
    c_wts = lambda s: jnp.clip(s - 1, 0, n - 1)
    c_mm2 = lambda s: jnp.clip(s - 2, 0, n - 1)
    once = pl.Buffered(1)
    row_blk = pl.BlockSpec((PEER_HEADS, PEER_ROWS, tt), lambda s: (0, c_wts(s) % nj, c_wts(s) // nj))
    all_blk = pl.BlockSpec((PEER_HEADS, PEER_NKEYS, tt), lambda s: (0, 0, c_wts(s) // nj), pipeline_mode=once)
    return pl.pallas_call(
        functools.partial(_peer_dense_kernel, nj=nj),
        grid=(n + 2,),
        in_specs=[
            pl.BlockSpec((d, tt), lambda s: (0, c_mm1(s) // nj), pipeline_mode=once),
            pl.BlockSpec((ec, d), lambda s: (c_mm1(s) % nj, 0)),
            pl.BlockSpec((d, ec), lambda s: (0, c_mm2(s) % nj)),
            row_blk, row_blk, all_blk, all_blk,
            pl.BlockSpec((PEER_HEADS, tt), lambda s: (0, c_wts(s) // nj)),
        ],
        out_specs=pl.BlockSpec((tt, d), lambda s: (c_mm2(s) // nj, 0)),
        out_shape=jax.ShapeDtypeStruct((t, d), f32),
        scratch_shapes=[pltpu.VMEM((ec, tt), f32), pltpu.VMEM((ec, tt), f32), pltpu.VMEM((ec, tt), bf16),
                        pltpu.VMEM((ec, tt), bf16), pltpu.VMEM((d, tt), f32)],
        compiler_params=_cp("arbitrary"),
        name="peer_dense",
    )(xnt, u, vt, a1, e1, a2, e2, tau)


def _final_norm_kernel(h_ref, p_ref, g_ref, o_ref):
    o_ref[...] = _rms_rows(h_ref[...] + p_ref[...], g_ref[...])


def final_norm(h, p, g, *, tm=256):
    t, d = h.shape
    tm = min(tm, t)
    assert t % tm == 0
    blk = pl.BlockSpec((tm, d), lambda i: (i, 0))
    return pl.pallas_call(
        _final_norm_kernel,
        grid=(t // tm,),
        in_specs=[blk, blk, pl.BlockSpec((1, d), lambda i: (0, 0))],
        out_specs=blk,
        out_shape=jax.ShapeDtypeStruct((t, d), f32),
        compiler_params=_cp("parallel"),
        name="final_norm",
    )(h, p, g.reshape(1, d))


def _token_tail(x2, z, o_r, o_d, o_m, wts):
    mixed = gated_mix(o_r, o_d, o_m, z, wts["w_up_ret"], wts["w_up_diff"], wts["w_up_mem"])
    h = matmul_residual(x2, mixed, wts["w_out"])
    pq, xn = norm_matmul(h, wts["norm_ffn_w"], wts["peer_w_q"], tm=512, tn=wts["peer_w_q"].shape[1], with_rows=True)
    a1, a2, e1, e2, tau = peer_select(pq, wts["peer_keys"])
    peer = peer_dense(xn.T, wts["peer_u"], wts["peer_vt"], a1, a2, e1, e2, tau)
    return final_norm(h, peer, wts["norm_final_w"])


def kernel(x_prompt, x_sample, mem_prompt, state_ret, cache_diff_k, cache_diff_v, cache_mem_k, cache_mem_v, page_table, norm_mix_w, norm_mem_w, norm_ffn_w, norm_final_w, w_in, w_mem_k, w_mem_v, diff_lambda_q1, diff_lambda_k1, diff_lambda_q2, diff_lambda_k2, diff_subln_w, rel_bias, w_up_ret, w_up_diff, w_up_mem, w_out, peer_w_q, peer_keys, peer_u, peer_v):
    b, s, d = x_prompt.shape
    db, ds, _ = x_sample.shape
    n_pages = page_table.shape[1]
    page = cache_diff_k.shape[2]
    past = n_pages * page
    n_mem = mem_prompt.shape[1]
    assert w_in.shape[0] == 1 and d == D_MODEL, "single-layer trunk"
    l = 0
    lam_init = _lambda_init(l)

    wts = {
        "w_up_ret": w_up_ret[l].astype(bf16), "w_up_diff": w_up_diff[l].astype(bf16),
        "w_up_mem": w_up_mem[l].astype(bf16), "w_out": w_out[l].astype(bf16),
        "norm_ffn_w": norm_ffn_w[l], "peer_w_q": peer_w_q[l].astype(bf16), "peer_keys": peer_keys[l],
        "peer_u": peer_u[l].astype(bf16), "peer_vt": peer_v[l].astype(bf16).T, "norm_final_w": norm_final_w,
    }
    w_in_b = w_in[l].astype(bf16)
    lam = diff_lambda(diff_lambda_q1[l], diff_lambda_k1[l], diff_lambda_q2[l], diff_lambda_k2[l], lam_init)

    xp = x_prompt.reshape(b * s, d)
    zp = norm_matmul(xp, norm_mix_w[l], w_in_b)
    w_mem = jnp.concatenate([w_mem_k[l], w_mem_v[l]], axis=1).astype(bf16)
    mkv = norm_matmul(mem_prompt.reshape(b * n_mem, d), norm_mem_w[l], w_mem)
    o_r, st_p = retention_prompt(zp, b, s)
    o_d = diff_attn_prompt(zp, lam, rel_bias, diff_subln_w[l], b, s, lam_init)
    o_m = mem_attn_prompt(zp, mkv, b, s, n_mem)
    y_prompt = _token_tail(xp, zp, o_r, o_d, o_m, wts).reshape(b, s, d)

    xs = x_sample.reshape(db * ds, d)
    zs = norm_matmul(xs, norm_mix_w[l], w_in_b)
    zs3 = zs.reshape(db, ds, IN_WIDTH)
    o_r_s, st_s = retention_sample(zs, state_ret[l], db, ds, past)
    ck = jnp.transpose(cache_diff_k[l], (0, 2, 3, 1))
    cv = cache_diff_v[l].reshape(-1, page * DIFF_HEADS, 2 * DIFF_HD)
    o_d_s = diff_attn_sample(zs3, ck, cv, page_table, lam, rel_bias, diff_subln_w[l], lam_init)
    o_m_s = mem_attn_sample(zs3, cache_mem_k[l].reshape(db, n_mem * MEM_HEADS, MEM_HD),
                            cache_mem_v[l].reshape(db, n_mem * MEM_HEADS, MEM_HD))
    y_sample = _token_tail(xs, zs, o_r_s, o_d_s.reshape(db * ds, DIFF_V_W), o_m_s.reshape(db * ds, MEM_W),
                           wts).reshape(db, ds, d)

    nh2 = 2 * DIFF_HEADS
    return (
        y_prompt,
        y_sample,
        st_p[None],
        zp[:, OFF_DK:OFF_DK + DIFF_QK_W].reshape(1, b, s, nh2, DIFF_HD),
        zp[:, OFF_DV:OFF_DV + DIFF_V_W].reshape(1, b, s, DIFF_HEADS, 2 * DIFF_HD),
        mkv[:, :MEM_W].reshape(1, b, n_mem, MEM_HEADS, MEM_HD),
        mkv[:, MEM_W:].reshape(1, b, n_mem, MEM_HEADS, MEM_HD),
        st_s[None],
        zs[:, OFF_DK:OFF_DK + DIFF_QK_W].reshape(1, db, ds, nh2, DIFF_HD),
        zs[:, OFF_DV:OFF_DV + DIFF_V_W].reshape(1, db, ds, DIFF_HEADS, 2 * DIFF_HD),
    )
```

```python
import functools
import math

import numpy as np
import jax
import jax.numpy as jnp
from jax import lax
from jax.experimental import pallas as pl
from jax.experimental.pallas import tpu as pltpu

f32 = jnp.float32
bf16 = jnp.bfloat16

D_MODEL = 2048
RET_HEADS = 8
RET_DK = 128
RET_DV = 256
RET_CHUNK = 128
ROPE_BASE = 10000.0
DIFF_HEADS = 8
DIFF_HD = 64
REL_BUCKETS = 32
REL_MAX_DIST = 128
MEM_HEADS = 4
MEM_HD = 128
PEER_HEADS = 8
PEER_NKEYS = 128
PEER_TOPK = 16
PEER_DK = 128
EPS = 1e-6
NEG = -1e30

RET_QK_W = RET_HEADS * RET_DK
RET_V_W = RET_HEADS * RET_DV
DIFF_QK_W = 2 * DIFF_HEADS * DIFF_HD
DIFF_V_W = DIFF_HEADS * 2 * DIFF_HD
MEM_W = MEM_HEADS * MEM_HD
OFF_RQ = 0
OFF_RK = OFF_RQ + RET_QK_W
OFF_RV = OFF_RK + RET_QK_W
OFF_RG = OFF_RV + RET_V_W
OFF_DQ = OFF_RG + RET_V_W
OFF_DK = OFF_DQ + DIFF_QK_W
OFF_DV = OFF_DK + DIFF_QK_W
OFF_MQ = OFF_DV + DIFF_V_W
OFF_GA = OFF_MQ + MEM_W
OFF_GB = OFF_GA + D_MODEL
OFF_GC = OFF_GB + D_MODEL
IN_WIDTH = OFF_GC + D_MODEL

VMEM_LIMIT_BYTES = 56 * 1024 * 1024

NT = (((1,), (1,)), ((), ()))
TN = (((0,), (0,)), ((), ()))


def _cp(*sem, flags=None):
    return pltpu.CompilerParams(dimension_semantics=sem, vmem_limit_bytes=VMEM_LIMIT_BYTES, flags=flags)


def _dot(a, b):
    return jnp.dot(a, b, preferred_element_type=f32)


def _dot_nt(a, b):
    return lax.dot_general(a, b, NT, preferred_element_type=f32)


def _dot_tn(a, b):
    return lax.dot_general(a, b, TN, preferred_element_type=f32)


def _rms_rows(x, g):
    y = x * lax.rsqrt(jnp.mean(x * x, axis=-1, keepdims=True) + EPS)
    return y * g


def _norm_matmul_kernel(x_ref, g_ref, w_ref, o_ref, *rest, rows):
    xn_ref = rest[-1]
    xn_out_ref = rest[0] if len(rest) == 2 else None

    @pl.when(pl.program_id(1) == 0)
    def _():
        def body(r, c):
            sl = pl.ds(pl.multiple_of(r * rows, rows), rows)
            xn = _rms_rows(x_ref[sl, :], g_ref[...]).astype(bf16)
            xn_ref[sl, :] = xn
            if xn_out_ref is not None:
                xn_out_ref[sl, :] = xn
            return c
        lax.fori_loop(0, x_ref.shape[0] // rows, body, 0)

    o_ref[...] = _dot(xn_ref[...], w_ref[...])


def norm_matmul(x, g, w, *, tm=1024, tn=512, with_rows=False):
    t, d = x.shape
    n = w.shape[1]
    tm = min(tm, t)
    assert t % tm == 0 and n % tn == 0 and tm % 16 == 0
    rows = 32 if tm % 32 == 0 else 16
    out_specs = [pl.BlockSpec((tm, tn), lambda i, j: (i, j))]
    out_shape = [jax.ShapeDtypeStruct((t, n), f32)]
    if with_rows:
        out_specs.append(pl.BlockSpec((tm, d), lambda i, j: (i, 0)))
        out_shape.append(jax.ShapeDtypeStruct((t, d), bf16))
    out = pl.pallas_call(
        functools.partial(_norm_matmul_kernel, rows=rows),
        grid=(t // tm, n // tn),
        in_specs=[
            pl.BlockSpec((tm, d), lambda i, j: (i, 0), pipeline_mode=pl.Buffered(1)),
            pl.BlockSpec((1, d), lambda i, j: (0, 0)),
            pl.BlockSpec((d, tn), lambda i, j: (0, j)),
        ],
        out_specs=out_specs,
        out_shape=out_shape,
        scratch_shapes=[pltpu.VMEM((tm, d), bf16)],
        compiler_params=_cp("parallel", "arbitrary"),
        name="norm_matmul",
    )(x, g.reshape(1, d), w)
    return out if with_rows else out[0]


def _ret_log_gamma():
    return jnp.log1p(-jnp.exp2(-5.0 - jnp.arange(RET_HEADS, dtype=f32)))


def _rope_tables(pos):
    half = RET_DK // 2
    inv = 1.0 / (ROPE_BASE ** jnp.linspace(0.0, 1.0, half, dtype=f32))
    ang = pos.astype(f32)[:, None] * inv[None, :]
    c, s = jnp.cos(ang), jnp.sin(ang)
    return jnp.concatenate([c, c], axis=-1), jnp.concatenate([-s, s], axis=-1)


def _rope(x, cos2, sin2):
    return x * cos2 + pltpu.roll(x, RET_DK // 2, 1) * sin2


def _ret_out(o, g):
    on = o * lax.rsqrt(jnp.mean(o * o, axis=-1, keepdims=True) + EPS)
    return (on * (g * jax.nn.sigmoid(g))).astype(bf16)


def _ret_prompt_kernel(decay_ref, q_ref, k_ref, v_ref, g_ref, cos_ref, sin_ref, dmask_ref, cs_ref, kdec_ref,
                       o_ref, st_out_ref, st_ref):
    c = pl.program_id(1)

    @pl.when(c == 0)
    def _():
        st_ref[...] = jnp.zeros_like(st_ref)

    cos2 = cos_ref[...]
    sin2 = sin_ref[...]
    for h in range(RET_HEADS):
        ks = slice(h * RET_DK, (h + 1) * RET_DK)
        vs = slice(h * RET_DV, (h + 1) * RET_DV)
        q = _rope(q_ref[:, ks], cos2, sin2)
        k = _rope(k_ref[:, ks], cos2, sin2) * (RET_DK ** -0.5)
        v = v_ref[:, vs]
        st = st_ref[h]
        qk = _dot_nt(q, k) * dmask_ref[h]
        o = _dot(qk, v) + _dot(q, st) * cs_ref[h]
        st_ref[h] = decay_ref[h] * st + _dot_tn(k * kdec_ref[h], v)
        o_ref[:, vs] = _ret_out(o, g_ref[:, vs])

    @pl.when(c == pl.num_programs(1) - 1)
    def _():
        st_out_ref[...] = st_ref[...]


def retention_prompt(z, batch, seq):
    cl = RET_CHUNK
    assert seq % cl == 0
    nc = seq // cl
    lg = _ret_log_gamma()[:, None]
    idx = jnp.arange(cl, dtype=f32)
    rel = idx[:, None] - idx[None, :]
    dmask = jnp.where(rel >= 0, jnp.exp(lg[:, :, None] * jnp.maximum(rel, 0.0)), 0.0)
    cs = jnp.broadcast_to(jnp.exp(lg * (idx + 1.0))[:, :, None], (RET_HEADS, cl, RET_DV))
    kdec = jnp.broadcast_to(jnp.exp(lg * (cl - 1.0 - idx))[:, :, None], (RET_HEADS, cl, RET_DK))
    decay = jnp.exp(lg[:, 0] * cl)
    cos2, sin2 = _rope_tables(jnp.arange(seq, dtype=jnp.int32))
    row = lambda b, c: b * nc + c
    const3 = lambda b, c: (0, 0, 0)
    return pl.pallas_call(
        _ret_prompt_kernel,
        grid=(batch, nc),
        in_specs=[
            pl.BlockSpec(memory_space=pltpu.SMEM),
            pl.BlockSpec((cl, RET_QK_W), lambda b, c: (row(b, c), OFF_RQ // RET_QK_W)),
            pl.BlockSpec((cl, RET_QK_W), lambda b, c: (row(b, c), OFF_RK // RET_QK_W)),
            pl.BlockSpec((cl, RET_V_W), lambda b, c: (row(b, c), OFF_RV // RET_V_W)),
            pl.BlockSpec((cl, RET_V_W), lambda b, c: (row(b, c), OFF_RG // RET_V_W)),
            pl.BlockSpec((cl, RET_DK), lambda b, c: (c, 0)),
            pl.BlockSpec((cl, RET_DK), lambda b, c: (c, 0)),
            pl.BlockSpec((RET_HEADS, cl, cl), const3),
            pl.BlockSpec((RET_HEADS, cl, RET_DV), const3),
            pl.BlockSpec((RET_HEADS, cl, RET_DK), const3),
        ],
        out_specs=[
            pl.BlockSpec((cl, RET_V_W), lambda b, c: (row(b, c), 0)),
            pl.BlockSpec((None, RET_HEADS, RET_DK, RET_DV), lambda b, c: (b, 0, 0, 0)),
        ],
        out_shape=[
            jax.ShapeDtypeStruct((batch * seq, RET_V_W), bf16),
            jax.ShapeDtypeStruct((batch, RET_HEADS, RET_DK, RET_DV), f32),
        ],
        scratch_shapes=[pltpu.VMEM((RET_HEADS, RET_DK, RET_DV), f32)],
        compiler_params=_cp("parallel", "arbitrary"),
        name="retention_prompt",
    )(decay, z, z, z, z, cos2, sin2, dmask, cs, kdec)


RET_SAMPLE_GROUP = 4


def _ret_sample_kernel(decay_ref, q_ref, k_ref, v_ref, g_ref, st_in_ref, cos_ref, sin_ref, dmask_ref, cs_ref,
                       kdec_ref, o_ref, st_out_ref, *, ds):
    cos2 = cos_ref[...]
    sin2 = sin_ref[...]
    rows = q_ref.shape[0]
    seq_of_row = lax.broadcasted_iota(jnp.int32, (rows, RET_DV), 0) // ds
    for h in range(RET_HEADS):
        ks = slice(h * RET_DK, (h + 1) * RET_DK)
        vs = slice(h * RET_DV, (h + 1) * RET_DV)
        q = _rope(q_ref[:, ks], cos2, sin2)
        k = _rope(k_ref[:, ks], cos2, sin2) * (RET_DK ** -0.5)
        v = v_ref[:, vs]
        kd = k * kdec_ref[h]
        qk = _dot_nt(q, k) * dmask_ref[h]
        cross = jnp.zeros((rows, RET_DV), f32)
        for s in range(RET_SAMPLE_GROUP):
            st = st_in_ref[s, h]
            cross = jnp.where(seq_of_row == s, _dot(q, st), cross)
            vb = jnp.where(seq_of_row == s, v, 0.0)
            st_out_ref[s, h] = decay_ref[h] * st + _dot_tn(kd, vb)
        o = _dot(qk, v) + cross * cs_ref[h]
        o_ref[:, vs] = _ret_out(o, g_ref[:, vs])


def retention_sample(z, state0, n_seq, ds, past):
    g = RET_SAMPLE_GROUP
    assert n_seq % g == 0 and (g * ds) % 8 == 0 and (ds % RET_CHUNK != 0)
    rows = g * ds
    lg = _ret_log_gamma()[:, None]
    idx = jnp.arange(ds, dtype=f32)
    rel = idx[:, None] - idx[None, :]
    dm = jnp.where(rel >= 0, jnp.exp(lg[:, :, None] * jnp.maximum(rel, 0.0)), 0.0)
    same_seq = jnp.kron(jnp.eye(g, dtype=f32), jnp.ones((ds, ds), f32))
    dmask = jnp.tile(dm, (1, g, g)) * same_seq[None]
    cs = jnp.broadcast_to(jnp.tile(jnp.exp(lg * (idx + 1.0)), (1, g))[:, :, None], (RET_HEADS, rows, RET_DV))
    kdec = jnp.broadcast_to(jnp.tile(jnp.exp(lg * (ds - 1.0 - idx)), (1, g))[:, :, None], (RET_HEADS, rows, RET_DK))
    decay = jnp.exp(lg[:, 0] * ds)
    cos2, sin2 = _rope_tables(past + jnp.arange(ds, dtype=jnp.int32))
    cos2, sin2 = jnp.tile(cos2, (g, 1)), jnp.tile(sin2, (g, 1))
    const2 = lambda i: (0, 0)
    const3 = lambda i: (0, 0, 0)
    st_spec = pl.BlockSpec((g, RET_HEADS, RET_DK, RET_DV), lambda i: (i, 0, 0, 0))
    return pl.pallas_call(
        functools.partial(_ret_sample_kernel, ds=ds),
        grid=(n_seq // g,),
        in_specs=[
            pl.BlockSpec(memory_space=pltpu.SMEM),
            pl.BlockSpec((rows, RET_QK_W), lambda i: (i, OFF_RQ // RET_QK_W)),
            pl.BlockSpec((rows, RET_QK_W), lambda i: (i, OFF_RK // RET_QK_W)),
            pl.BlockSpec((rows, RET_V_W), lambda i: (i, OFF_RV // RET_V_W)),
            pl.BlockSpec((rows, RET_V_W), lambda i: (i, OFF_RG // RET_V_W)),
            st_spec,
            pl.BlockSpec((rows, RET_DK), const2),
            pl.BlockSpec((rows, RET_DK), const2),
            pl.BlockSpec((RET_HEADS, rows, rows), const3),
            pl.BlockSpec((RET_HEADS, rows, RET_DV), const3),
            pl.BlockSpec((RET_HEADS, rows, RET_DK), const3),
        ],
        out_specs=[pl.BlockSpec((rows, RET_V_W), lambda i: (i, 0)), st_spec],
        out_shape=[
            jax.ShapeDtypeStruct((n_seq * ds, RET_V_W), bf16),
            jax.ShapeDtypeStruct((n_seq, RET_HEADS, RET_DK, RET_DV), f32),
        ],
        compiler_params=_cp("parallel"),
        name="retention_sample",
    )(decay, z, z, z, z, state0, cos2, sin2, dmask, cs, kdec)


def _lambda_init(layer):
    return 0.8 - 0.6 * math.exp(-0.3 * layer)


def _lambda_kernel(q1_ref, k1_ref, q2_ref, k2_ref, o_ref, *, lam_init):
    a = jnp.sum(q1_ref[...] * k1_ref[...], axis=-1, keepdims=True)
    b = jnp.sum(q2_ref[...] * k2_ref[...], axis=-1, keepdims=True)
    o_ref[...] = jnp.broadcast_to(jnp.exp(a) - jnp.exp(b) + lam_init, o_ref.shape)


def diff_lambda(q1, k1, q2, k2, lam_init):
    r = lambda a: a.reshape(1, DIFF_HD)
    return pl.pallas_call(
        functools.partial(_lambda_kernel, lam_init=lam_init),
        out_shape=jax.ShapeDtypeStruct((1, 2 * DIFF_HD), f32),
        name="diff_lambda",
    )(r(q1), r(k1), r(q2), r(k2))


def _rel_bucket_np(rel):
    n = np.maximum(rel, 0)
    max_exact = REL_BUCKETS // 2
    nf = np.maximum(n, 1).astype(np.float32)
    large = max_exact + (np.log(nf / np.float32(max_exact)) / np.float32(math.log(REL_MAX_DIST / max_exact))
                         * np.float32(REL_BUCKETS - max_exact)).astype(np.int32)
    large = np.minimum(large, REL_BUCKETS - 1)
    return np.where(n < max_exact, n, large).astype(np.int32)


def _bias_expand_kernel(rb_ref, tbl_ref, o_ref):
    h = pl.program_id(0)
    tbl = tbl_ref[...]
    acc = jnp.full(tbl.shape, NEG, f32)
    for bkt in range(REL_BUCKETS):
        acc = jnp.where(tbl == bkt, rb_ref[bkt, h], acc)
    o_ref[...] = acc


def bias_expand(bucket_tbl, rel_bias):
    r, c = bucket_tbl.shape
    return pl.pallas_call(
        _bias_expand_kernel,
        grid=(DIFF_HEADS,),
        in_specs=[pl.BlockSpec(memory_space=pltpu.SMEM), pl.BlockSpec((r, c), lambda h: (0, 0))],
        out_specs=pl.BlockSpec((None, r, c), lambda h: (h, 0, 0)),
        out_shape=jax.ShapeDtypeStruct((DIFF_HEADS, r, c), f32),
        compiler_params=_cp("parallel"),
        name="bias_expand",
    )(rel_bias.astype(f32), bucket_tbl)


def _subln_out(o, w, lam_init):
    on = o * lax.rsqrt(jnp.mean(o * o, axis=-1, keepdims=True) + EPS)
    return (on * w * (1.0 - lam_init)).astype(bf16)


def _softmax_step(s, v, m_old, l_old, acc_old):
    m_new = jnp.maximum(m_old, jnp.max(s, axis=-1, keepdims=True))
    p = jnp.exp(s - m_new)
    alpha = jnp.exp(m_old - m_new)
    return m_new, alpha * l_old + jnp.sum(p, axis=-1, keepdims=True), alpha * acc_old + _dot(p, v)


DIFF_TQ = 256
DIFF_TK = 512
DIFF_HP = 2


def _diff_prompt_kernel(lam_ref, q_ref, k_ref, v_ref, bias_ref, w_ref, o_ref, vt_ref, m_ref, l_ref, acc_ref, *,
                        lam_init):
    qi = pl.program_id(2)
    tq, tk, w2 = DIFF_TQ, DIFF_TK, 2 * DIFF_HD
    heads = [slice(hh * w2, (hh + 1) * w2) for hh in range(DIFF_HP)]

    @pl.when(qi == 0)
    def _():
        for hh, hs in enumerate(heads):
            vt_ref[hh] = v_ref[:, hs].T.astype(bf16)

    lane = lax.broadcasted_iota(jnp.int32, (tq, w2), 1)
    qm = []
    for hs in heads:
        q = q_ref[:, hs] * (DIFF_HD ** -0.5)
        qm.append((jnp.where(lane < DIFF_HD, q, 0.0).astype(bf16), jnp.where(lane >= DIFF_HD, q, 0.0).astype(bf16)))
    m_ref[...] = jnp.full(m_ref.shape, NEG, f32)
    l_ref[...] = jnp.zeros_like(l_ref)
    acc_ref[...] = jnp.zeros_like(acc_ref)

    def body(kb, carry):
        sl = pl.ds(pl.multiple_of(kb * tk, tk), tk)
        dist = jnp.minimum(qi - (tk // tq) * kb, bias_ref.shape[1] - 1)
        chains = [(hh, mp) for hh in range(DIFF_HP) for mp in range(2)]
        kblk = [k_ref[sl, hs].astype(bf16) for hs in heads]
        s = [_dot_nt(kblk[hh], qm[hh][mp]) + bias_ref[hh, dist] for hh, mp in chains]
        m_old = [m_ref[hh, mp] for hh, mp in chains]
        m_new = [jnp.maximum(mo, jnp.max(sc, axis=0, keepdims=True)) for mo, sc in zip(m_old, s)]
        p = [jnp.exp(sc - mn) for sc, mn in zip(s, m_new)]
        alpha = [jnp.exp(mo - mn) for mo, mn in zip(m_old, m_new)]
        for c, (hh, mp) in enumerate(chains):
            l_ref[hh, mp] = alpha[c] * l_ref[hh, mp] + jnp.sum(p[c], axis=0, keepdims=True)
            m_ref[hh, mp] = m_new[c]
        pv = [_dot(vt_ref[hh, :, sl], p[c].astype(bf16)) for c, (hh, mp) in enumerate(chains)]
        for c, (hh, mp) in enumerate(chains):
            acc_ref[hh, mp] = alpha[c] * acc_ref[hh, mp] + pv[c]
        return carry

    lax.fori_loop(0, qi // (tk // tq) + 1, body, 0)
    for hh, hs in enumerate(heads):
        o0 = (acc_ref[hh, 0] / l_ref[hh, 0]).T
        o1 = (acc_ref[hh, 1] / l_ref[hh, 1]).T
        o_ref[:, hs] = _subln_out(o0 - lam_ref[...] * o1, w_ref[...], lam_init)


def diff_attn_prompt(z, lam, rel_bias, subln_w, batch, seq, lam_init):
    tq, tk, hp = DIFF_TQ, DIFF_TK, DIFF_HP
    assert seq % tk == 0 and tk % tq == 0 and DIFF_HEADS % hp == 0
    nq = seq // tq
    kk = np.arange(tk)[:, None]
    qq = np.arange(tq)[None, :]
    tbs = []
    dist = 0
    while True:
        rel = dist * tq + qq - kk
        tb = np.where(rel >= 0, _rel_bucket_np(rel), -1)
        if rel.min() >= 0 and (tb == tb[0, 0]).all() and (_rel_bucket_np(np.arange(rel.min(), seq + 1)) == tb[0, 0]).all():
            tbs.append(tb)
            break
        tbs.append(tb)
        dist += 1
    nd = len(tbs)
    tbl = jnp.asarray(np.concatenate(tbs, axis=0).astype(np.int32))
    bias = bias_expand(tbl, rel_bias).reshape(DIFF_HEADS, nd, tk, tq)
    w2 = 2 * DIFF_HD
    wide = hp * w2
    return pl.pallas_call(
        functools.partial(_diff_prompt_kernel, lam_init=lam_init),
        grid=(batch, DIFF_HEADS // hp, nq),
        in_specs=[
            pl.BlockSpec((1, w2), lambda b, h, i: (0, 0)),
            pl.BlockSpec((tq, wide), lambda b, h, i: (b * nq + i, OFF_DQ // wide + h)),
            pl.BlockSpec((seq, wide), lambda b, h, i: (b, OFF_DK // wide + h)),
            pl.BlockSpec((seq, wide), lambda b, h, i: (b, OFF_DV // wide + h)),
            pl.BlockSpec((hp, nd, tk, tq), lambda b, h, i: (h, 0, 0, 0)),
            pl.BlockSpec((1, w2), lambda b, h, i: (0, 0)),
        ],
        out_specs=pl.BlockSpec((tq, wide), lambda b, h, i: (b * nq + i, h)),
        out_shape=jax.ShapeDtypeStruct((batch * seq, DIFF_V_W), bf16),
        scratch_shapes=[pltpu.VMEM((hp, w2, seq), bf16), pltpu.VMEM((hp, 2, 1, tq), f32),
                        pltpu.VMEM((hp, 2, 1, tq), f32), pltpu.VMEM((hp, 2, w2, tq), f32)],
        compiler_params=_cp("parallel", "parallel", "arbitrary"),
        name="diff_attn_prompt",
    )(lam, z, z, z, bias, subln_w.reshape(1, w2).astype(f32))


DIFF_SAMPLE_PAGES = 8


def _diff_sample_kernel(pt_ref, lam_ref, q_ref, kn_ref, vn_ref, *rest, ds, pg, lam_init):
    k_refs, v_refs = rest[:pg], rest[pg:2 * pg]
    bias_ref, biasn_ref, w_ref, o_ref, qbd_ref, kpad_ref, vpad_ref, m_ref, l_ref, acc_ref = rest[2 * pg:]
    b = pl.program_id(0)
    p = pl.program_id(1)
    w2 = 2 * DIFF_HD
    hr = 2 * ds

    @pl.when((b == 0) & (p == 0))
    def _():
        kpad_ref[...] = jnp.zeros_like(kpad_ref)
        vpad_ref[...] = jnp.zeros_like(vpad_ref)

    @pl.when(p == 0)
    def _():
        lane = lax.broadcasted_iota(jnp.int32, (ds, w2), 1)
        for h in range(DIFF_HEADS):
            qh = q_ref[:, h * w2:(h + 1) * w2] * (DIFF_HD ** -0.5)
            qbd_ref[h * hr:h * hr + ds, :] = jnp.where(lane < DIFF_HD, qh, 0.0)
            qbd_ref[h * hr + ds:(h + 1) * hr, :] = jnp.where(lane >= DIFF_HD, qh, 0.0)
        kpad_ref[0:ds, :] = kn_ref[...]
        vpad_ref[0:ds, :] = vn_ref[...]
        m_ref[...] = jnp.full(m_ref.shape, NEG, f32)
        l_ref[...] = jnp.zeros_like(l_ref)
        acc_ref[...] = jnp.zeros_like(acc_ref)

    head_rows = [slice(h * hr, (h + 1) * hr) for h in range(DIFF_HEADS)]

    def attend(scores, values, bias):
        s = jnp.concatenate([scores(h) for h in range(DIFF_HEADS)], axis=0) + bias
        m_old = m_ref[...]
        m_new = jnp.maximum(m_old, jnp.max(s, axis=-1, keepdims=True))
        p = jnp.exp(s - m_new)
        alpha = jnp.exp(m_old - m_new)
        l_ref[...] = alpha * l_ref[...] + jnp.sum(p, axis=-1, keepdims=True)
        m_ref[...] = m_new
        pv = jnp.concatenate([_dot(p[head_rows[h], :], values(h)) for h in range(DIFF_HEADS)], axis=0)
        acc_ref[...] = alpha * acc_ref[...] + pv

    page = k_refs[0].shape[-1]

    def page_scores(h):
        kt = jnp.concatenate([k_refs[i][2 * h:2 * h + 2].reshape(w2, page) for i in range(pg)], axis=1)
        return _dot(qbd_ref[head_rows[h], :], kt)

    def page_values(h):
        return jnp.concatenate([v_refs[i][pl.ds(h, page, stride=DIFF_HEADS), :] for i in range(pg)], axis=0)

    attend(page_scores, page_values, bias_ref[p])

    @pl.when(p == pl.num_programs(1) - 1)
    def _():
        attend(lambda h: _dot_nt(qbd_ref[head_rows[h], :], kpad_ref[:, h * w2:(h + 1) * w2]),
               lambda h: vpad_ref[:, h * w2:(h + 1) * w2], biasn_ref[...])
        for h in range(DIFF_HEADS):
            hs = slice(h * w2, (h + 1) * w2)
            rs = head_rows[h]
            blk = acc_ref[rs, :] / l_ref[rs, :]
            o = blk[0:ds] - lam_ref[...] * blk[ds:hr]
            o_ref[:, hs] = _subln_out(o, w_ref[...], lam_init)


def diff_attn_sample(z3, cache_k, cache_v, page_table, lam, rel_bias, subln_w, lam_init):
    n_seq, ds, _ = z3.shape
    n_pages = page_table.shape[1]
    page = cache_k.shape[-1]
    past = n_pages * page
    rows = 2 * DIFF_HEADS * ds
    pg = math.gcd(n_pages, DIFF_SAMPLE_PAGES)
    n_steps = n_pages // pg
    assert page == 128 and 2 * ds == 8
    qi = np.arange(ds)
    rel = past + qi[None, :, None] - np.arange(past).reshape(n_steps, 1, pg * page)
    tb = np.broadcast_to(_rel_bucket_np(rel)[:, None], (n_steps, 2, ds, pg * page))
    rel_new = qi[:, None] - qi[None, :]
    tbn = np.full((2, ds, page), -1, np.int32)
    tbn[:, :, :ds] = np.where(rel_new >= 0, _rel_bucket_np(rel_new), -1)[None]
    bias = bias_expand(jnp.asarray(tb.reshape(n_steps * 2 * ds, pg * page)), rel_bias)
    bias = bias.reshape(DIFF_HEADS, n_steps, 2 * ds, pg * page).transpose(1, 0, 2, 3).reshape(n_steps, rows, pg * page)
    biasn = bias_expand(jnp.asarray(tbn.reshape(2 * ds, page)), rel_bias).reshape(rows, page)
    w2 = 2 * DIFF_HD
    page_spec = lambda shape, i: pl.BlockSpec(
        (None,) + shape, lambda b, p, pt: (pt[b * n_pages + p * pg + i],) + (0,) * len(shape))
    grid_spec = pltpu.PrefetchScalarGridSpec(
        num_scalar_prefetch=1,
        grid=(n_seq, n_steps),
        in_specs=[
            pl.BlockSpec((1, w2), lambda b, p, pt: (0, 0)),
            pl.BlockSpec((None, ds, DIFF_QK_W), lambda b, p, pt: (b, 0, OFF_DQ // DIFF_QK_W)),
            pl.BlockSpec((None, ds, DIFF_QK_W), lambda b, p, pt: (b, 0, OFF_DK // DIFF_QK_W)),
            pl.BlockSpec((None, ds, DIFF_V_W), lambda b, p, pt: (b, 0, OFF_DV // DIFF_V_W)),
            *[page_spec(cache_k.shape[1:], i) for i in range(pg)],
            *[page_spec(cache_v.shape[1:], i) for i in range(pg)],
            pl.BlockSpec((n_steps, rows, pg * page), lambda b, p, pt: (0, 0, 0)),
            pl.BlockSpec((rows, page), lambda b, p, pt: (0, 0)),
            pl.BlockSpec((1, w2), lambda b, p, pt: (0, 0)),
        ],
        out_specs=pl.BlockSpec((None, ds, DIFF_V_W), lambda b, p, pt: (b, 0, 0)),
        scratch_shapes=[
            pltpu.VMEM((rows, w2), f32),
            pltpu.VMEM((page, DIFF_QK_W), f32),
            pltpu.VMEM((page, DIFF_V_W), f32),
            pltpu.VMEM((rows, 1), f32),
            pltpu.VMEM((rows, 1), f32),
            pltpu.VMEM((rows, w2), f32),
        ],
    )
    return pl.pallas_call(
        functools.partial(_diff_sample_kernel, ds=ds, pg=pg, lam_init=lam_init),
        grid_spec=grid_spec,
        out_shape=jax.ShapeDtypeStruct((n_seq, ds, DIFF_V_W), bf16),
        compiler_params=_cp("arbitrary", "arbitrary"),
        name="diff_attn_sample",
    )(page_table.reshape(-1), lam, z3, z3, z3, *([cache_k] * pg), *([cache_v] * pg), bias, biasn,
      subln_w.reshape(1, w2).astype(f32))


def _mem_attn_head(q, mk, mv):
    s = _dot_nt(q, mk) * (MEM_HD ** -0.5)
    m = jnp.max(s, axis=-1, keepdims=True)
    p = jnp.exp(s - m)
    return (_dot(p, mv) / jnp.sum(p, axis=-1, keepdims=True)).astype(bf16)


def _mem_attn_kernel(q_ref, mk_ref, mv_ref, o_ref):
    for h in range(MEM_HEADS):
        hs = slice(h * MEM_HD, (h + 1) * MEM_HD)
        o_ref[:, hs] = _mem_attn_head(q_ref[:, hs], mk_ref[:, hs], mv_ref[:, hs])


def _mem_attn_sample_kernel(q_ref, mk_ref, mv_ref, o_ref):
    n_mem = mk_ref.shape[0] // MEM_HEADS
    for h in range(MEM_HEADS):
        hs = slice(h * MEM_HD, (h + 1) * MEM_HD)
        rows = pl.ds(h, n_mem, stride=MEM_HEADS)
        o_ref[:, hs] = _mem_attn_head(q_ref[:, hs], mk_ref[rows, :], mv_ref[rows, :])


def mem_attn_prompt(z, mkv, batch, seq, n_mem, *, tq=256):
    assert seq % tq == 0
    nq = seq // tq
    return pl.pallas_call(
        _mem_attn_kernel,
        grid=(batch, nq),
        in_specs=[
            pl.BlockSpec((tq, MEM_W), lambda b, i: (b * nq + i, OFF_MQ // MEM_W)),
            pl.BlockSpec((n_mem, MEM_W), lambda b, i: (b, 0)),
            pl.BlockSpec((n_mem, MEM_W), lambda b, i: (b, 1)),
        ],
        out_specs=pl.BlockSpec((tq, MEM_W), lambda b, i: (b * nq + i, 0)),
        out_shape=jax.ShapeDtypeStruct((batch * seq, MEM_W), bf16),
        compiler_params=_cp("parallel", "parallel"),
        name="mem_attn_prompt",
    )(z, mkv, mkv)


def mem_attn_sample(z3, mk, mv):
    n_seq, ds, _ = z3.shape
    mem_spec = pl.BlockSpec((None, mk.shape[1], MEM_HD), lambda b: (b, 0, 0))
    return pl.pallas_call(
        _mem_attn_sample_kernel,
        grid=(n_seq,),
        in_specs=[pl.BlockSpec((None, ds, MEM_W), lambda b: (b, 0, OFF_MQ // MEM_W)), mem_spec, mem_spec],
        out_specs=pl.BlockSpec((None, ds, MEM_W), lambda b: (b, 0, 0)),
        out_shape=jax.ShapeDtypeStruct((n_seq, ds, MEM_W), bf16),
        compiler_params=_cp("parallel"),
        name="mem_attn_sample",
    )(z3, mk, mv)


def _mix_kernel(orr_ref, od_ref, om_ref, ga_ref, gb_ref, gc_ref, wr_ref, wd_ref, wm_ref, o_ref):
    mixed = (jax.nn.sigmoid(ga_ref[...]) * _dot(orr_ref[...], wr_ref[...])
             + jax.nn.sigmoid(gb_ref[...]) * _dot(od_ref[...], wd_ref[...])
             + jax.nn.sigmoid(gc_ref[...]) * _dot(om_ref[...], wm_ref[...]))
    o_ref[...] = mixed.astype(bf16)


def gated_mix(o_r, o_d, o_m, z, w_r, w_d, w_m, *, tm=1024, tn=512):
    t = o_r.shape[0]
    tm = min(tm, t)
    assert t % tm == 0 and D_MODEL % tn == 0
    gate = lambda off: pl.BlockSpec((tm, tn), lambda i, j: (i, off // tn + j))
    act = lambda w: pl.BlockSpec((tm, w), lambda i, j: (i, 0))
    wgt = lambda w: pl.BlockSpec((w, tn), lambda i, j: (0, j))
    return pl.pallas_call(
        _mix_kernel,
        grid=(t // tm, D_MODEL // tn),
        in_specs=[act(RET_V_W), act(DIFF_V_W), act(MEM_W), gate(OFF_GA), gate(OFF_GB), gate(OFF_GC),
                  wgt(RET_V_W), wgt(DIFF_V_W), wgt(MEM_W)],
        out_specs=pl.BlockSpec((tm, tn), lambda i, j: (i, j)),
        out_shape=jax.ShapeDtypeStruct((t, D_MODEL), bf16),
        compiler_params=_cp("parallel", "parallel"),
        name="gated_mix",
    )(o_r, o_d, o_m, z, z, z, w_r, w_d, w_m)


def _matmul_residual_kernel(x_ref, a_ref, w_ref, o_ref):
    o_ref[...] = x_ref[...] + _dot(a_ref[...], w_ref[...])


def matmul_residual(x, a, w, *, tm=1024, tn=512):
    t, k = a.shape
    n = w.shape[1]
    tm = min(tm, t)
    assert t % tm == 0 and n % tn == 0
    return pl.pallas_call(
        _matmul_residual_kernel,
        grid=(t // tm, n // tn),
        in_specs=[
            pl.BlockSpec((tm, tn), lambda i, j: (i, j)),
            pl.BlockSpec((tm, k), lambda i, j: (i, 0)),
            pl.BlockSpec((k, tn), lambda i, j: (0, j)),
        ],
        out_specs=pl.BlockSpec((tm, tn), lambda i, j: (i, j)),
        out_shape=jax.ShapeDtypeStruct((t, n), f32),
        compiler_params=_cp("parallel", "parallel"),
        name="matmul_residual",
    )(x, a, w)


_PEER_NTOP = PEER_TOPK + 1
_PEER_CAND = [(a, b) for a in range(_PEER_NTOP) for b in range(_PEER_NTOP) if (a + 1) * (b + 1) <= _PEER_NTOP]


def _top_values(x, k):
    out = []
    for i in range(k):
        mx = jnp.max(x, axis=0, keepdims=True)
        out.append(mx)
        if i + 1 < k:
            x = jnp.where(x == mx, -jnp.inf, x)
    return out


def _peer_select_kernel(pq_ref, keys_ref, thr_ref, a2_ref, e1_ref, e2_ref):
    for h in range(PEER_HEADS):
        sc = []
        for c in range(2):
            off = (2 * h + c) * PEER_DK
            sc.append(_dot_nt(keys_ref[h, c], pq_ref[:, off:off + PEER_DK]))
        t1 = _top_values(sc[0], _PEER_NTOP)
        t2 = _top_values(sc[1], _PEER_NTOP)
        cand = jnp.concatenate([t1[a] + t2[b] for a, b in _PEER_CAND], axis=0)
        top = _top_values(cand, _PEER_NTOP)
        m = top[0]
        zsum = jnp.ones_like(m)
        for r in range(1, PEER_TOPK):
            zsum = zsum + jnp.exp(top[r] - m)
        tau = 0.5 * (top[PEER_TOPK - 1] + top[PEER_TOPK])
        thr_ref[h] = tau - sc[0]
        a2_ref[h] = sc[1]
        e1_ref[h] = jnp.exp(sc[0] - t1[0]) / zsum
        e2_ref[h] = jnp.exp(sc[1] - t2[0])


def peer_select(pq, keys, *, tt=256):
    t = pq.shape[0]
    tt = min(tt, t)
    assert t % tt == 0
    big = pl.BlockSpec((PEER_HEADS, PEER_NKEYS, tt), lambda i: (0, 0, i))
    shp = jax.ShapeDtypeStruct((PEER_HEADS, PEER_NKEYS, t), f32)
    return pl.pallas_call(
        _peer_select_kernel,
        grid=(t // tt,),
        in_specs=[
            pl.BlockSpec((tt, PEER_HEADS * 2 * PEER_DK), lambda i: (i, 0)),
            pl.BlockSpec((PEER_HEADS, 2, PEER_NKEYS, PEER_DK), lambda i: (0, 0, 0, 0)),
        ],
        out_specs=[big, big, big, big],
        out_shape=[shp, shp, shp, shp],
        compiler_params=_cp("parallel"),
        name="peer_select",
    )(pq, keys)


PEER_ROWS = 8


def _gelu(x):
    return 0.5 * x * (1.0 + lax.erf(x * (2.0 ** -0.5)))


PEER_SUB = 16


def _peer_weights_block(c0, i0, ht_ref, wt_ref, thr_ref, e1_ref, a2_ref, e2_ref):
    cs = slice(c0, c0 + 128)
    isl = slice(i0, i0 + PEER_SUB)
    g = [None] * PEER_ROWS
    for h in range(PEER_HEADS):
        a2 = a2_ref[h, isl, cs]
        e2 = e2_ref[h, isl, cs]
        for r in range(PEER_ROWS):
            w = e1_ref[h, r:r + 1, cs] * e2
            sel = jnp.where(a2 >= thr_ref[h, r:r + 1, cs], w, 0.0)
            g[r] = sel if g[r] is None else g[r] + sel
    out = []
    for r in range(PEER_ROWS):
        rs = slice(r * PEER_NKEYS + i0, r * PEER_NKEYS + i0 + PEER_SUB)
        out.append((rs, cs, (g[r] * _gelu(ht_ref[rs, cs])).astype(bf16)))

    def store():
        for rs, cs_, val in out:
            wt_ref[rs, cs_] = val
    return store


def _spread(n_items, n_slots):
    return [range(-(-s * n_items // n_slots), -(-(s + 1) * n_items // n_slots)) for s in range(n_slots)]


def _peer_dense_kernel(xnt_ref, u_ref, vt_ref, thr_ref, e1_ref, a2_ref, e2_ref, o_ref,
                       ht0_ref, ht1_ref, wt0_ref, wt1_ref, acc_ref, *, nj):
    s = pl.program_id(0)
    j_out = lax.rem(s + (nj - 2), nj)

    @pl.when(s == 0)
    def _():
        for ref in (ht0_ref, ht1_ref, wt0_ref, wt1_ref, acc_ref):
            ref[...] = jnp.zeros_like(ref)

    @pl.when((s >= 2) & (j_out == 0))
    def _():
        acc_ref[...] = jnp.zeros_like(acc_ref)

    ec, tt = ht0_ref.shape
    d = acc_ref.shape[0]
    mt = 256
    nt = min(256, tt)

    def step(ht_w, ht_r, wt_w, wt_r):
        def mm_scores(m0, n0):
            val = _dot(u_ref[m0:m0 + mt, :], xnt_ref[:, n0:n0 + nt])

            def store():
                ht_w[m0:m0 + mt, n0:n0 + nt] = val
            return store

        def mm_out(m0, n0):
            val = acc_ref[m0:m0 + mt, n0:n0 + nt] + _dot(vt_ref[m0:m0 + mt, :], wt_r[:, n0:n0 + nt])

            def store():
                acc_ref[m0:m0 + mt, n0:n0 + nt] = val
            return store

        pieces = [(mm_scores, m0, n0) for m0 in range(0, ec, mt) for n0 in range(0, tt, nt)]
        pieces += [(mm_out, m0, n0) for m0 in range(0, d, mt) for n0 in range(0, tt, nt)]
        n1 = (ec // mt) * (tt // nt)
        pieces = [p for _, _, p in sorted(
            ((k + 0.5) / n1 if k < n1 else (k - n1 + 0.5) / (len(pieces) - n1), k, p)
            for k, p in enumerate(pieces))]
        blocks = [(c0, i0) for c0 in range(0, tt, 128) for i0 in range(0, PEER_NKEYS, PEER_SUB)]
        for (c0, i0), todo in zip(blocks, _spread(len(pieces), len(blocks))):
            stores = [pieces[k][0](*pieces[k][1:]) for k in todo]
            stores.append(_peer_weights_block(c0, i0, ht_r, wt_w, thr_ref, e1_ref, a2_ref, e2_ref))
            for store in stores:
                store()

    even = lax.rem(s, 2) == 0

    @pl.when(even)
    def _():
        step(ht0_ref, ht1_ref, wt1_ref, wt0_ref)

    @pl.when(jnp.logical_not(even))
    def _():
        step(ht1_ref, ht0_ref, wt0_ref, wt1_ref)

    @pl.when((s >= 2) & (j_out == nj - 1))
    def _():
        o_ref[...] = acc_ref[...].T


def peer_dense(xnt, u, vt, thr, a2, e1, e2, *, tt=512):
    d, t = xnt.shape
    n_exp = u.shape[0]
    tt = min(tt, t)
    ec = PEER_ROWS * PEER_NKEYS
    assert t % tt == 0 and n_exp % ec == 0 and tt % 128 == 0
    nj = n_exp // ec
    n = (t // tt) * nj
    assert nj > 2
    c_mm1 = lambda s: jnp.minimum(s, n - 1)
    c_wts = lambda s: jnp.clip(s - 1, 0, n - 1)
    c_mm2 = lambda s: jnp.clip(s - 2, 0, n - 1)
    once = pl.Buffered(1)
    row_blk = pl.BlockSpec((PEER_HEADS, PEER_ROWS, tt), lambda s: (0, c_wts(s) % nj, c_wts(s) // nj))
    all_blk = pl.BlockSpec((PEER_HEADS, PEER_NKEYS, tt), lambda s: (0, 0, c_wts(s) // nj), pipeline_mode=once)
    return pl.pallas_call(
        functools.partial(_peer_dense_kernel, nj=nj),
        grid=(n + 2,),
        in_specs=[
            pl.BlockSpec((d, tt), lambda s: (0, c_mm1(s) // nj), pipeline_mode=once),
            pl.BlockSpec((ec, d), lambda s: (c_mm1(s) % nj, 0)),
            pl.BlockSpec((d, ec), lambda s: (0, c_mm2(s) % nj)),
            row_blk, row_blk, all_blk, all_blk,
        ],
        out_specs=pl.BlockSpec((tt, d), lambda s: (c_mm2(s) // nj, 0)),
        out_shape=jax.ShapeDtypeStruct((t, d), f32),
        scratch_shapes=[pltpu.VMEM((ec, tt), f32), pltpu.VMEM((ec, tt), f32), pltpu.VMEM((ec, tt), bf16),
                        pltpu.VMEM((ec, tt), bf16), pltpu.VMEM((d, tt), f32)],
        compiler_params=_cp("arbitrary"),
        name="peer_dense",
    )(xnt, u, vt, thr, e1, a2, e2)


def _final_norm_kernel(h_ref, p_ref, g_ref, o_ref):
    o_ref[...] = _rms_rows(h_ref[...] + p_ref[...], g_ref[...])


def final_norm(h, p, g, *, tm=256):
    t, d = h.shape
    tm = min(tm, t)
    assert t % tm == 0
    blk = pl.BlockSpec((tm, d), lambda i: (i, 0))
    return pl.pallas_call(
        _final_norm_kernel,
        grid=(t // tm,),
        in_specs=[blk, blk, pl.BlockSpec((1, d), lambda i: (0, 0))],
        out_specs=blk,
        out_shape=jax.ShapeDtypeStruct((t, d), f32),
        compiler_params=_cp("parallel"),
        name="final_norm",
    )(h, p, g.reshape(1, d))


def _token_tail(x2, z, o_r, o_d, o_m, wts):
    mixed = gated_mix(o_r, o_d, o_m, z, wts["w_up_ret"], wts["w_up_diff"], wts["w_up_mem"])
    h = matmul_residual(x2, mixed, wts["w_out"])
    pq, xn = norm_matmul(h, wts["norm_ffn_w"], wts["peer_w_q"], with_rows=True)
    thr, a2, e1, e2 = peer_select(pq, wts["peer_keys"])
    peer = peer_dense(xn.T, wts["peer_u"], wts["peer_vt"], thr, a2, e1, e2)
    return final_norm(h, peer, wts["norm_final_w"])


def kernel(x_prompt, x_sample, mem_prompt, state_ret, cache_diff_k, cache_diff_v, cache_mem_k, cache_mem_v, page_table, norm_mix_w, norm_mem_w, norm_ffn_w, norm_final_w, w_in, w_mem_k, w_mem_v, diff_lambda_q1, diff_lambda_k1, diff_lambda_q2, diff_lambda_k2, diff_subln_w, rel_bias, w_up_ret, w_up_diff, w_up_mem, w_out, peer_w_q, peer_keys, peer_u, peer_v):
    b, s, d = x_prompt.shape
    db, ds, _ = x_sample.shape
    n_pages = page_table.shape[1]
    page = cache_diff_k.shape[2]
    past = n_pages * page
    n_mem = mem_prompt.shape[1]
    assert w_in.shape[0] == 1 and d == D_MODEL, "single-layer trunk"
    l = 0
    lam_init = _lambda_init(l)

    wts = {
        "w_up_ret": w_up_ret[l].astype(bf16), "w_up_diff": w_up_diff[l].astype(bf16),
        "w_up_mem": w_up_mem[l].astype(bf16), "w_out": w_out[l].astype(bf16),
        "norm_ffn_w": norm_ffn_w[l], "peer_w_q": peer_w_q[l].astype(bf16), "peer_keys": peer_keys[l],
        "peer_u": peer_u[l].astype(bf16), "peer_vt": peer_v[l].astype(bf16).T, "norm_final_w": norm_final_w,
    }
    w_in_b = w_in[l].astype(bf16)
    lam = diff_lambda(diff_lambda_q1[l], diff_lambda_k1[l], diff_lambda_q2[l], diff_lambda_k2[l], lam_init)

    xp = x_prompt.reshape(b * s, d)
    zp = norm_matmul(xp, norm_mix_w[l], w_in_b)
    w_mem = jnp.concatenate([w_mem_k[l], w_mem_v[l]], axis=1).astype(bf16)
    mkv = norm_matmul(mem_prompt.reshape(b * n_mem, d), norm_mem_w[l], w_mem)
    o_r, st_p = retention_prompt(zp, b, s)
    o_d = diff_attn_prompt(zp, lam, rel_bias, diff_subln_w[l], b, s, lam_init)
    o_m = mem_attn_prompt(zp, mkv, b, s, n_mem)
    y_prompt = _token_tail(xp, zp, o_r, o_d, o_m, wts).reshape(b, s, d)

    xs = x_sample.reshape(db * ds, d)
    zs = norm_matmul(xs, norm_mix_w[l], w_in_b)
    zs3 = zs.reshape(db, ds, IN_WIDTH)
    o_r_s, st_s = retention_sample(zs, state_ret[l], db, ds, past)
    ck = jnp.transpose(cache_diff_k[l], (0, 2, 3, 1))
    cv = cache_diff_v[l].reshape(-1, page * DIFF_HEADS, 2 * DIFF_HD)
    o_d_s = diff_attn_sample(zs3, ck, cv, page_table, lam, rel_bias, diff_subln_w[l], lam_init)
    o_m_s = mem_attn_sample(zs3, cache_mem_k[l].reshape(db, n_mem * MEM_HEADS, MEM_HD),
                            cache_mem_v[l].reshape(db, n_mem * MEM_HEADS, MEM_HD))
    y_sample = _token_tail(xs, zs, o_r_s, o_d_s.reshape(db * ds, DIFF_V_W), o_m_s.reshape(db * ds, MEM_W),
                           wts).reshape(db, ds, d)

    nh2 = 2 * DIFF_HEADS
    return (
        y_prompt,
        y_sample,
        st_p[None],
        zp[:, OFF_DK:OFF_DK + DIFF_QK_W].reshape(1, b, s, nh2, DIFF_HD),
        zp[:, OFF_DV:OFF_DV + DIFF_V_W].reshape(1, b, s, DIFF_HEADS, 2 * DIFF_HD),
        mkv[:, :MEM_W].reshape(1, b, n_mem, MEM_HEADS, MEM_HD),
        mkv[:, MEM_W:].reshape(1, b, n_mem, MEM_HEADS, MEM_HD),
        st_s[None],
        zs[:, OFF_DK:OFF_DK + DIFF_QK_W].reshape(1, db, ds, nh2, DIFF_HD),
        zs[:, OFF_DV:OFF_DV + DIFF_V_W].reshape(1, db, ds, DIFF_HEADS, 2 * DIFF_HD),
    )
```

```python
import functools
import math

import numpy as np
import jax
import jax.numpy as jnp
from jax import lax
from jax.experimental import pallas as pl
from jax.experimental.pallas import tpu as pltpu

f32 = jnp.float32
bf16 = jnp.bfloat16

D_MODEL = 2048
RET_HEADS = 8
RET_DK = 128
RET_DV = 256
RET_CHUNK = 128
ROPE_BASE = 10000.0
DIFF_HEADS = 8
DIFF_HD = 64
REL_BUCKETS = 32
REL_MAX_DIST = 128
MEM_HEADS = 4
MEM_HD = 128
PEER_HEADS = 8
PEER_NKEYS = 128
PEER_TOPK = 16
PEER_DK = 128
EPS = 1e-6
NEG = -1e30

RET_QK_W = RET_HEADS * RET_DK
RET_V_W = RET_HEADS * RET_DV
DIFF_QK_W = 2 * DIFF_HEADS * DIFF_HD
DIFF_V_W = DIFF_HEADS * 2 * DIFF_HD
MEM_W = MEM_HEADS * MEM_HD
OFF_RQ = 0
OFF_RK = OFF_RQ + RET_QK_W
OFF_RV = OFF_RK + RET_QK_W
OFF_RG = OFF_RV + RET_V_W
OFF_DQ = OFF_RG + RET_V_W
OFF_DK = OFF_DQ + DIFF_QK_W
OFF_DV = OFF_DK + DIFF_QK_W
OFF_MQ = OFF_DV + DIFF_V_W
OFF_GA = OFF_MQ + MEM_W
OFF_GB = OFF_GA + D_MODEL
OFF_GC = OFF_GB + D_MODEL
IN_WIDTH = OFF_GC + D_MODEL

VMEM_LIMIT_BYTES = 56 * 1024 * 1024

NT = (((1,), (1,)), ((), ()))
TN = (((0,), (0,)), ((), ()))


def _cp(*sem, flags=None):
    return pltpu.CompilerParams(dimension_semantics=sem, vmem_limit_bytes=VMEM_LIMIT_BYTES, flags=flags)


def _dot(a, b):
    return jnp.dot(a, b, preferred_element_type=f32)


def _dot_nt(a, b):
    return lax.dot_general(a, b, NT, preferred_element_type=f32)


def _dot_tn(a, b):
    return lax.dot_general(a, b, TN, preferred_element_type=f32)


def _rms_rows(x, g):
    y = x * lax.rsqrt(jnp.mean(x * x, axis=-1, keepdims=True) + EPS)
    return y * g


def _norm_matmul_kernel(x_ref, g_ref, w_ref, o_ref, *rest, rows):
    xn_ref = rest[-1]
    xn_out_ref = rest[0] if len(rest) == 2 else None

    @pl.when(pl.program_id(1) == 0)
    def _():
        def body(r, c):
            sl = pl.ds(pl.multiple_of(r * rows, rows), rows)
            xn = _rms_rows(x_ref[sl, :], g_ref[...]).astype(bf16)
            xn_ref[sl, :] = xn
            if xn_out_ref is not None:
                xn_out_ref[sl, :] = xn
            return c
        lax.fori_loop(0, x_ref.shape[0] // rows, body, 0)

    o_ref[...] = _dot(xn_ref[...], w_ref[...])


def norm_matmul(x, g, w, *, tm=1024, tn=512, with_rows=False):
    t, d = x.shape
    n = w.shape[1]
    tm = min(tm, t)
    assert t % tm == 0 and n % tn == 0 and tm % 16 == 0
    rows = 32 if tm % 32 == 0 else 16
    out_specs = [pl.BlockSpec((tm, tn), lambda i, j: (i, j))]
    out_shape = [jax.ShapeDtypeStruct((t, n), f32)]
    if with_rows:
        out_specs.append(pl.BlockSpec((tm, d), lambda i, j: (i, 0)))
        out_shape.append(jax.ShapeDtypeStruct((t, d), bf16))
    out = pl.pallas_call(
        functools.partial(_norm_matmul_kernel, rows=rows),
        grid=(t // tm, n // tn),
        in_specs=[
            pl.BlockSpec((tm, d), lambda i, j: (i, 0), pipeline_mode=pl.Buffered(1)),
            pl.BlockSpec((1, d), lambda i, j: (0, 0)),
            pl.BlockSpec((d, tn), lambda i, j: (0, j)),
        ],
        out_specs=out_specs,
        out_shape=out_shape,
        scratch_shapes=[pltpu.VMEM((tm, d), bf16)],
        compiler_params=_cp("parallel", "arbitrary"),
        name="norm_matmul",
    )(x, g.reshape(1, d), w)
    return out if with_rows else out[0]


def _ret_log_gamma():
    return jnp.log1p(-jnp.exp2(-5.0 - jnp.arange(RET_HEADS, dtype=f32)))


def _rope_tables(pos):
    half = RET_DK // 2
    inv = 1.0 / (ROPE_BASE ** jnp.linspace(0.0, 1.0, half, dtype=f32))
    ang = pos.astype(f32)[:, None] * inv[None, :]
    c, s = jnp.cos(ang), jnp.sin(ang)
    return jnp.concatenate([c, c], axis=-1), jnp.concatenate([-s, s], axis=-1)


def _rope(x, cos2, sin2):
    return x * cos2 + pltpu.roll(x, RET_DK // 2, 1) * sin2


def _ret_out(o, g):
    on = o * lax.rsqrt(jnp.mean(o * o, axis=-1, keepdims=True) + EPS)
    return (on * (g * jax.nn.sigmoid(g))).astype(bf16)


def _ret_prompt_kernel(decay_ref, q_ref, k_ref, v_ref, g_ref, cos_ref, sin_ref, dmask_ref, cs_ref, kdec_ref,
                       o_ref, st_out_ref, st_ref):
    c = pl.program_id(1)

    @pl.when(c == 0)
    def _():
        st_ref[...] = jnp.zeros_like(st_ref)

    cos2 = cos_ref[...]
    sin2 = sin_ref[...]
    for h in range(RET_HEADS):
        ks = slice(h * RET_DK, (h + 1) * RET_DK)
        vs = slice(h * RET_DV, (h + 1) * RET_DV)
        q = _rope(q_ref[:, ks], cos2, sin2)
        k = _rope(k_ref[:, ks], cos2, sin2) * (RET_DK ** -0.5)
        v = v_ref[:, vs]
        st = st_ref[h]
        qk = _dot_nt(q, k) * dmask_ref[h]
        o = _dot(qk, v) + _dot(q, st) * cs_ref[h]
        st_ref[h] = decay_ref[h] * st + _dot_tn(k * kdec_ref[h], v)
        o_ref[:, vs] = _ret_out(o, g_ref[:, vs])

    @pl.when(c == pl.num_programs(1) - 1)
    def _():
        st_out_ref[...] = st_ref[...]


def retention_prompt(z, batch, seq):
    cl = RET_CHUNK
    assert seq % cl == 0
    nc = seq // cl
    lg = _ret_log_gamma()[:, None]
    idx = jnp.arange(cl, dtype=f32)
    rel = idx[:, None] - idx[None, :]
    dmask = jnp.where(rel >= 0, jnp.exp(lg[:, :, None] * jnp.maximum(rel, 0.0)), 0.0)
    cs = jnp.broadcast_to(jnp.exp(lg * (idx + 1.0))[:, :, None], (RET_HEADS, cl, RET_DV))
    kdec = jnp.broadcast_to(jnp.exp(lg * (cl - 1.0 - idx))[:, :, None], (RET_HEADS, cl, RET_DK))
    decay = jnp.exp(lg[:, 0] * cl)
    cos2, sin2 = _rope_tables(jnp.arange(seq, dtype=jnp.int32))
    row = lambda b, c: b * nc + c
    const3 = lambda b, c: (0, 0, 0)
    return pl.pallas_call(
        _ret_prompt_kernel,
        grid=(batch, nc),
        in_specs=[
            pl.BlockSpec(memory_space=pltpu.SMEM),
            pl.BlockSpec((cl, RET_QK_W), lambda b, c: (row(b, c), OFF_RQ // RET_QK_W)),
            pl.BlockSpec((cl, RET_QK_W), lambda b, c: (row(b, c), OFF_RK // RET_QK_W)),
            pl.BlockSpec((cl, RET_V_W), lambda b, c: (row(b, c), OFF_RV // RET_V_W)),
            pl.BlockSpec((cl, RET_V_W), lambda b, c: (row(b, c), OFF_RG // RET_V_W)),
            pl.BlockSpec((cl, RET_DK), lambda b, c: (c, 0)),
            pl.BlockSpec((cl, RET_DK), lambda b, c: (c, 0)),
            pl.BlockSpec((RET_HEADS, cl, cl), const3),
            pl.BlockSpec((RET_HEADS, cl, RET_DV), const3),
            pl.BlockSpec((RET_HEADS, cl, RET_DK), const3),
        ],
        out_specs=[
            pl.BlockSpec((cl, RET_V_W), lambda b, c: (row(b, c), 0)),
            pl.BlockSpec((None, RET_HEADS, RET_DK, RET_DV), lambda b, c: (b, 0, 0, 0)),
        ],
        out_shape=[
            jax.ShapeDtypeStruct((batch * seq, RET_V_W), bf16),
            jax.ShapeDtypeStruct((batch, RET_HEADS, RET_DK, RET_DV), f32),
        ],
        scratch_shapes=[pltpu.VMEM((RET_HEADS, RET_DK, RET_DV), f32)],
        compiler_params=_cp("parallel", "arbitrary"),
        name="retention_prompt",
    )(decay, z, z, z, z, cos2, sin2, dmask, cs, kdec)


RET_SAMPLE_GROUP = 4


def _ret_sample_kernel(decay_ref, q_ref, k_ref, v_ref, g_ref, st_in_ref, cos_ref, sin_ref, dmask_ref, cs_ref,
                       kdec_ref, o_ref, st_out_ref, *, ds):
    cos2 = cos_ref[...]
    sin2 = sin_ref[...]
    rows = q_ref.shape[0]
    seq_of_row = lax.broadcasted_iota(jnp.int32, (rows, RET_DV), 0) // ds
    for h in range(RET_HEADS):
        ks = slice(h * RET_DK, (h + 1) * RET_DK)
        vs = slice(h * RET_DV, (h + 1) * RET_DV)
        q = _rope(q_ref[:, ks], cos2, sin2)
        k = _rope(k_ref[:, ks], cos2, sin2) * (RET_DK ** -0.5)
        v = v_ref[:, vs]
        kd = k * kdec_ref[h]
        qk = _dot_nt(q, k) * dmask_ref[h]
        cross = jnp.zeros((rows, RET_DV), f32)
        for s in range(RET_SAMPLE_GROUP):
            st = st_in_ref[s, h]
            cross = jnp.where(seq_of_row == s, _dot(q, st), cross)
            vb = jnp.where(seq_of_row == s, v, 0.0)
            st_out_ref[s, h] = decay_ref[h] * st + _dot_tn(kd, vb)
        o = _dot(qk, v) + cross * cs_ref[h]
        o_ref[:, vs] = _ret_out(o, g_ref[:, vs])


def retention_sample(z, state0, n_seq, ds, past):
    g = RET_SAMPLE_GROUP
    assert n_seq % g == 0 and (g * ds) % 8 == 0 and (ds % RET_CHUNK != 0)
    rows = g * ds
    lg = _ret_log_gamma()[:, None]
    idx = jnp.arange(ds, dtype=f32)
    rel = idx[:, None] - idx[None, :]
    dm = jnp.where(rel >= 0, jnp.exp(lg[:, :, None] * jnp.maximum(rel, 0.0)), 0.0)
    same_seq = jnp.kron(jnp.eye(g, dtype=f32), jnp.ones((ds, ds), f32))
    dmask = jnp.tile(dm, (1, g, g)) * same_seq[None]
    cs = jnp.broadcast_to(jnp.tile(jnp.exp(lg * (idx + 1.0)), (1, g))[:, :, None], (RET_HEADS, rows, RET_DV))
    kdec = jnp.broadcast_to(jnp.tile(jnp.exp(lg * (ds - 1.0 - idx)), (1, g))[:, :, None], (RET_HEADS, rows, RET_DK))
    decay = jnp.exp(lg[:, 0] * ds)
    cos2, sin2 = _rope_tables(past + jnp.arange(ds, dtype=jnp.int32))
    cos2, sin2 = jnp.tile(cos2, (g, 1)), jnp.tile(sin2, (g, 1))
    const2 = lambda i: (0, 0)
    const3 = lambda i: (0, 0, 0)
    st_spec = pl.BlockSpec((g, RET_HEADS, RET_DK, RET_DV), lambda i: (i, 0, 0, 0))
    return pl.pallas_call(
        functools.partial(_ret_sample_kernel, ds=ds),
        grid=(n_seq // g,),
        in_specs=[
            pl.BlockSpec(memory_space=pltpu.SMEM),
            pl.BlockSpec((rows, RET_QK_W), lambda i: (i, OFF_RQ // RET_QK_W)),
            pl.BlockSpec((rows, RET_QK_W), lambda i: (i, OFF_RK // RET_QK_W)),
            pl.BlockSpec((rows, RET_V_W), lambda i: (i, OFF_RV // RET_V_W)),
            pl.BlockSpec((rows, RET_V_W), lambda i: (i, OFF_RG // RET_V_W)),
            st_spec,
            pl.BlockSpec((rows, RET_DK), const2),
            pl.BlockSpec((rows, RET_DK), const2),
            pl.BlockSpec((RET_HEADS, rows, rows), const3),
            pl.BlockSpec((RET_HEADS, rows, RET_DV), const3),
            pl.BlockSpec((RET_HEADS, rows, RET_DK), const3),
        ],
        out_specs=[pl.BlockSpec((rows, RET_V_W), lambda i: (i, 0)), st_spec],
        out_shape=[
            jax.ShapeDtypeStruct((n_seq * ds, RET_V_W), bf16),
            jax.ShapeDtypeStruct((n_seq, RET_HEADS, RET_DK, RET_DV), f32),
        ],
        compiler_params=_cp("parallel"),
        name="retention_sample",
    )(decay, z, z, z, z, state0, cos2, sin2, dmask, cs, kdec)


def _lambda_init(layer):
    return 0.8 - 0.6 * math.exp(-0.3 * layer)


def _lambda_kernel(q1_ref, k1_ref, q2_ref, k2_ref, o_ref, *, lam_init):
    a = jnp.sum(q1_ref[...] * k1_ref[...], axis=-1, keepdims=True)
    b = jnp.sum(q2_ref[...] * k2_ref[...], axis=-1, keepdims=True)
    o_ref[...] = jnp.broadcast_to(jnp.exp(a) - jnp.exp(b) + lam_init, o_ref.shape)


def diff_lambda(q1, k1, q2, k2, lam_init):
    r = lambda a: a.reshape(1, DIFF_HD)
    return pl.pallas_call(
        functools.partial(_lambda_kernel, lam_init=lam_init),
        out_shape=jax.ShapeDtypeStruct((1, 2 * DIFF_HD), f32),
        name="diff_lambda",
    )(r(q1), r(k1), r(q2), r(k2))


def _rel_bucket_np(rel):
    n = np.maximum(rel, 0)
    max_exact = REL_BUCKETS // 2
    nf = np.maximum(n, 1).astype(np.float32)
    large = max_exact + (np.log(nf / np.float32(max_exact)) / np.float32(math.log(REL_MAX_DIST / max_exact))
                         * np.float32(REL_BUCKETS - max_exact)).astype(np.int32)
    large = np.minimum(large, REL_BUCKETS - 1)
    return np.where(n < max_exact, n, large).astype(np.int32)


def _bias_expand_kernel(rb_ref, tbl_ref, o_ref):
    h = pl.program_id(0)
    tbl = tbl_ref[...]
    acc = jnp.full(tbl.shape, NEG, f32)
    for bkt in range(REL_BUCKETS):
        acc = jnp.where(tbl == bkt, rb_ref[bkt, h], acc)
    o_ref[...] = acc


def bias_expand(bucket_tbl, rel_bias):
    r, c = bucket_tbl.shape
    return pl.pallas_call(
        _bias_expand_kernel,
        grid=(DIFF_HEADS,),
        in_specs=[pl.BlockSpec(memory_space=pltpu.SMEM), pl.BlockSpec((r, c), lambda h: (0, 0))],
        out_specs=pl.BlockSpec((None, r, c), lambda h: (h, 0, 0)),
        out_shape=jax.ShapeDtypeStruct((DIFF_HEADS, r, c), f32),
        compiler_params=_cp("parallel"),
        name="bias_expand",
    )(rel_bias.astype(f32), bucket_tbl)


def _subln_out(o, w, lam_init):
    on = o * lax.rsqrt(jnp.mean(o * o, axis=-1, keepdims=True) + EPS)
    return (on * w * (1.0 - lam_init)).astype(bf16)


def _softmax_step(s, v, m_old, l_old, acc_old):
    m_new = jnp.maximum(m_old, jnp.max(s, axis=-1, keepdims=True))
    p = jnp.exp(s - m_new)
    alpha = jnp.exp(m_old - m_new)
    return m_new, alpha * l_old + jnp.sum(p, axis=-1, keepdims=True), alpha * acc_old + _dot(p, v)


DIFF_TQ = 256
DIFF_TK = 512
DIFF_HP = 2


def _diff_prompt_kernel(lam_ref, q_ref, k_ref, v_ref, bias_ref, w_ref, o_ref, vt_ref, m_ref, l_ref, acc_ref, *,
                        lam_init):
    qi = pl.program_id(2)
    tq, tk, w2 = DIFF_TQ, DIFF_TK, 2 * DIFF_HD
    heads = [slice(hh * w2, (hh + 1) * w2) for hh in range(DIFF_HP)]

    @pl.when(qi == 0)
    def _():
        for hh, hs in enumerate(heads):
            vt_ref[hh] = v_ref[:, hs].T.astype(bf16)

    lane = lax.broadcasted_iota(jnp.int32, (tq, w2), 1)
    qm = []
    for hs in heads:
        q = q_ref[:, hs] * (DIFF_HD ** -0.5)
        qm.append((jnp.where(lane < DIFF_HD, q, 0.0).astype(bf16), jnp.where(lane >= DIFF_HD, q, 0.0).astype(bf16)))
    m_ref[...] = jnp.full(m_ref.shape, NEG, f32)
    l_ref[...] = jnp.zeros_like(l_ref)
    acc_ref[...] = jnp.zeros_like(acc_ref)

    def body(kb, carry):
        sl = pl.ds(pl.multiple_of(kb * tk, tk), tk)
        dist = jnp.minimum(qi - (tk // tq) * kb, bias_ref.shape[1] - 1)
        chains = [(hh, mp) for hh in range(DIFF_HP) for mp in range(2)]
        kblk = [k_ref[sl, hs].astype(bf16) for hs in heads]
        s = [_dot_nt(kblk[hh], qm[hh][mp]) + bias_ref[hh, dist] for hh, mp in chains]
        m_old = [m_ref[hh, mp] for hh, mp in chains]
        m_new = [jnp.maximum(mo, jnp.max(sc, axis=0, keepdims=True)) for mo, sc in zip(m_old, s)]
        p = [jnp.exp(sc - mn) for sc, mn in zip(s, m_new)]
        alpha = [jnp.exp(mo - mn) for mo, mn in zip(m_old, m_new)]
        for c, (hh, mp) in enumerate(chains):
            l_ref[hh, mp] = alpha[c] * l_ref[hh, mp] + jnp.sum(p[c], axis=0, keepdims=True)
            m_ref[hh, mp] = m_new[c]
        pv = [_dot(vt_ref[hh, :, sl], p[c].astype(bf16)) for c, (hh, mp) in enumerate(chains)]
        for c, (hh, mp) in enumerate(chains):
            acc_ref[hh, mp] = alpha[c] * acc_ref[hh, mp] + pv[c]
        return carry

    lax.fori_loop(0, qi // (tk // tq) + 1, body, 0)
    for hh, hs in enumerate(heads):
        o0 = (acc_ref[hh, 0] / l_ref[hh, 0]).T
        o1 = (acc_ref[hh, 1] / l_ref[hh, 1]).T
        o_ref[:, hs] = _subln_out(o0 - lam_ref[...] * o1, w_ref[...], lam_init)


def diff_attn_prompt(z, lam, rel_bias, subln_w, batch, seq, lam_init):
    tq, tk, hp = DIFF_TQ, DIFF_TK, DIFF_HP
    assert seq % tk == 0 and tk % tq == 0 and DIFF_HEADS % hp == 0
    nq = seq // tq
    kk = np.arange(tk)[:, None]
    qq = np.arange(tq)[None, :]
    tbs = []
    dist = 0
    while True:
        rel = dist * tq + qq - kk
        tb = np.where(rel >= 0, _rel_bucket_np(rel), -1)
        if rel.min() >= 0 and (tb == tb[0, 0]).all() and (_rel_bucket_np(np.arange(rel.min(), seq + 1)) == tb[0, 0]).all():
            tbs.append(tb)
            break
        tbs.append(tb)
        dist += 1
    nd = len(tbs)
    tbl = jnp.asarray(np.concatenate(tbs, axis=0).astype(np.int32))
    bias = bias_expand(tbl, rel_bias).reshape(DIFF_HEADS, nd, tk, tq)
    w2 = 2 * DIFF_HD
    wide = hp * w2
    return pl.pallas_call(
        functools.partial(_diff_prompt_kernel, lam_init=lam_init),
        grid=(batch, DIFF_HEADS // hp, nq),
        in_specs=[
            pl.BlockSpec((1, w2), lambda b, h, i: (0, 0)),
            pl.BlockSpec((tq, wide), lambda b, h, i: (b * nq + i, OFF_DQ // wide + h)),
            pl.BlockSpec((seq, wide), lambda b, h, i: (b, OFF_DK // wide + h)),
            pl.BlockSpec((seq, wide), lambda b, h, i: (b, OFF_DV // wide + h)),
            pl.BlockSpec((hp, nd, tk, tq), lambda b, h, i: (h, 0, 0, 0)),
            pl.BlockSpec((1, w2), lambda b, h, i: (0, 0)),
        ],
        out_specs=pl.BlockSpec((tq, wide), lambda b, h, i: (b * nq + i, h)),
        out_shape=jax.ShapeDtypeStruct((batch * seq, DIFF_V_W), bf16),
        scratch_shapes=[pltpu.VMEM((hp, w2, seq), bf16), pltpu.VMEM((hp, 2, 1, tq), f32),
                        pltpu.VMEM((hp, 2, 1, tq), f32), pltpu.VMEM((hp, 2, w2, tq), f32)],
        compiler_params=_cp("parallel", "parallel", "arbitrary"),
        name="diff_attn_prompt",
    )(lam, z, z, z, bias, subln_w.reshape(1, w2).astype(f32))


DIFF_SAMPLE_PAGES = 8


def _diff_sample_kernel(pt_ref, lam_ref, q_ref, kn_ref, vn_ref, *rest, ds, pg, lam_init):
    k_refs, v_refs = rest[:pg], rest[pg:2 * pg]
    bias_ref, biasn_ref, w_ref, o_ref, qbd_ref, kpad_ref, vpad_ref, m_ref, l_ref, acc_ref = rest[2 * pg:]
    b = pl.program_id(0)
    p = pl.program_id(1)
    w2 = 2 * DIFF_HD
    hr = 2 * ds

    @pl.when((b == 0) & (p == 0))
    def _():
        kpad_ref[...] = jnp.zeros_like(kpad_ref)
        vpad_ref[...] = jnp.zeros_like(vpad_ref)

    @pl.when(p == 0)
    def _():
        lane = lax.broadcasted_iota(jnp.int32, (ds, w2), 1)
        for h in range(DIFF_HEADS):
            qh = q_ref[:, h * w2:(h + 1) * w2] * (DIFF_HD ** -0.5)
            qbd_ref[h * hr:h * hr + ds, :] = jnp.where(lane < DIFF_HD, qh, 0.0)
            qbd_ref[h * hr + ds:(h + 1) * hr, :] = jnp.where(lane >= DIFF_HD, qh, 0.0)
        kpad_ref[0:ds, :] = kn_ref[...]
        vpad_ref[0:ds, :] = vn_ref[...]
        m_ref[...] = jnp.full(m_ref.shape, NEG, f32)
        l_ref[...] = jnp.zeros_like(l_ref)
        acc_ref[...] = jnp.zeros_like(acc_ref)

    head_rows = [slice(h * hr, (h + 1) * hr) for h in range(DIFF_HEADS)]

    def attend(scores, values, bias):
        s = jnp.concatenate([scores(h) for h in range(DIFF_HEADS)], axis=0) + bias
        m_old = m_ref[...]
        m_new = jnp.maximum(m_old, jnp.max(s, axis=-1, keepdims=True))
        p = jnp.exp(s - m_new)
        alpha = jnp.exp(m_old - m_new)
        l_ref[...] = alpha * l_ref[...] + jnp.sum(p, axis=-1, keepdims=True)
        m_ref[...] = m_new
        pv = jnp.concatenate([_dot(p[head_rows[h], :], values(h)) for h in range(DIFF_HEADS)], axis=0)
        acc_ref[...] = alpha * acc_ref[...] + pv

    page = k_refs[0].shape[-1]

    def page_scores(h):
        kt = jnp.concatenate([k_refs[i][2 * h:2 * h + 2].reshape(w2, page) for i in range(pg)], axis=1)
        return _dot(qbd_ref[head_rows[h], :], kt)

    def page_values(h):
        return jnp.concatenate([v_refs[i][pl.ds(h, page, stride=DIFF_HEADS), :] for i in range(pg)], axis=0)

    attend(page_scores, page_values, bias_ref[p])

    @pl.when(p == pl.num_programs(1) - 1)
    def _():
        attend(lambda h: _dot_nt(qbd_ref[head_rows[h], :], kpad_ref[:, h * w2:(h + 1) * w2]),
               lambda h: vpad_ref[:, h * w2:(h + 1) * w2], biasn_ref[...])
        for h in range(DIFF_HEADS):
            hs = slice(h * w2, (h + 1) * w2)
            rs = head_rows[h]
            blk = acc_ref[rs, :] / l_ref[rs, :]
            o = blk[0:ds] - lam_ref[...] * blk[ds:hr]
            o_ref[:, hs] = _subln_out(o, w_ref[...], lam_init)


def diff_attn_sample(z3, cache_k, cache_v, page_table, lam, rel_bias, subln_w, lam_init):
    n_seq, ds, _ = z3.shape
    n_pages = page_table.shape[1]
    page = cache_k.shape[-1]
    past = n_pages * page
    rows = 2 * DIFF_HEADS * ds
    pg = math.gcd(n_pages, DIFF_SAMPLE_PAGES)
    n_steps = n_pages // pg
    assert page == 128 and 2 * ds == 8
    qi = np.arange(ds)
    rel = past + qi[None, :, None] - np.arange(past).reshape(n_steps, 1, pg * page)
    tb = np.broadcast_to(_rel_bucket_np(rel)[:, None], (n_steps, 2, ds, pg * page))
    rel_new = qi[:, None] - qi[None, :]
    tbn = np.full((2, ds, page), -1, np.int32)
    tbn[:, :, :ds] = np.where(rel_new >= 0, _rel_bucket_np(rel_new), -1)[None]
    bias = bias_expand(jnp.asarray(tb.reshape(n_steps * 2 * ds, pg * page)), rel_bias)
    bias = bias.reshape(DIFF_HEADS, n_steps, 2 * ds, pg * page).transpose(1, 0, 2, 3).reshape(n_steps, rows, pg * page)
    biasn = bias_expand(jnp.asarray(tbn.reshape(2 * ds, page)), rel_bias).reshape(rows, page)
    w2 = 2 * DIFF_HD
    page_spec = lambda shape, i: pl.BlockSpec(
        (None,) + shape, lambda b, p, pt: (pt[b * n_pages + p * pg + i],) + (0,) * len(shape))
    grid_spec = pltpu.PrefetchScalarGridSpec(
        num_scalar_prefetch=1,
        grid=(n_seq, n_steps),
        in_specs=[
            pl.BlockSpec((1, w2), lambda b, p, pt: (0, 0)),
            pl.BlockSpec((None, ds, DIFF_QK_W), lambda b, p, pt: (b, 0, OFF_DQ // DIFF_QK_W)),
            pl.BlockSpec((None, ds, DIFF_QK_W), lambda b, p, pt: (b, 0, OFF_DK // DIFF_QK_W)),
            pl.BlockSpec((None, ds, DIFF_V_W), lambda b, p, pt: (b, 0, OFF_DV // DIFF_V_W)),
            *[page_spec(cache_k.shape[1:], i) for i in range(pg)],
            *[page_spec(cache_v.shape[1:], i) for i in range(pg)],
            pl.BlockSpec((n_steps, rows, pg * page), lambda b, p, pt: (0, 0, 0)),
            pl.BlockSpec((rows, page), lambda b, p, pt: (0, 0)),
            pl.BlockSpec((1, w2), lambda b, p, pt: (0, 0)),
        ],
        out_specs=pl.BlockSpec((None, ds, DIFF_V_W), lambda b, p, pt: (b, 0, 0)),
        scratch_shapes=[
            pltpu.VMEM((rows, w2), f32),
            pltpu.VMEM((page, DIFF_QK_W), f32),
            pltpu.VMEM((page, DIFF_V_W), f32),
            pltpu.VMEM((rows, 1), f32),
            pltpu.VMEM((rows, 1), f32),
            pltpu.VMEM((rows, w2), f32),
        ],
    )
    return pl.pallas_call(
        functools.partial(_diff_sample_kernel, ds=ds, pg=pg, lam_init=lam_init),
        grid_spec=grid_spec,
        out_shape=jax.ShapeDtypeStruct((n_seq, ds, DIFF_V_W), bf16),
        compiler_params=_cp("arbitrary", "arbitrary"),
        name="diff_attn_sample",
    )(page_table.reshape(-1), lam, z3, z3, z3, *([cache_k] * pg), *([cache_v] * pg), bias, biasn,
      subln_w.reshape(1, w2).astype(f32))


def _mem_attn_head(q, mk, mv):
    s = _dot_nt(q, mk) * (MEM_HD ** -0.5)
    m = jnp.max(s, axis=-1, keepdims=True)
    p = jnp.exp(s - m)
    return (_dot(p, mv) / jnp.sum(p, axis=-1, keepdims=True)).astype(bf16)


def _mem_attn_kernel(q_ref, mk_ref, mv_ref, o_ref):
    for h in range(MEM_HEADS):
        hs = slice(h * MEM_HD, (h + 1) * MEM_HD)
        o_ref[:, hs] = _mem_attn_head(q_ref[:, hs], mk_ref[:, hs], mv_ref[:, hs])


def _mem_attn_sample_kernel(q_ref, mk_ref, mv_ref, o_ref):
    n_mem = mk_ref.shape[0] // MEM_HEADS
    for h in range(MEM_HEADS):
        hs = slice(h * MEM_HD, (h + 1) * MEM_HD)
        rows = pl.ds(h, n_mem, stride=MEM_HEADS)
        o_ref[:, hs] = _mem_attn_head(q_ref[:, hs], mk_ref[rows, :], mv_ref[rows, :])


def mem_attn_prompt(z, mkv, batch, seq, n_mem, *, tq=256):
    assert seq % tq == 0
    nq = seq // tq
    return pl.pallas_call(
        _mem_attn_kernel,
        grid=(batch, nq),
        in_specs=[
            pl.BlockSpec((tq, MEM_W), lambda b, i: (b * nq + i, OFF_MQ // MEM_W)),
            pl.BlockSpec((n_mem, MEM_W), lambda b, i: (b, 0)),
            pl.BlockSpec((n_mem, MEM_W), lambda b, i: (b, 1)),
        ],
        out_specs=pl.BlockSpec((tq, MEM_W), lambda b, i: (b * nq + i, 0)),
        out_shape=jax.ShapeDtypeStruct((batch * seq, MEM_W), bf16),
        compiler_params=_cp("parallel", "parallel"),
        name="mem_attn_prompt",
    )(z, mkv, mkv)


def mem_attn_sample(z3, mk, mv):
    n_seq, ds, _ = z3.shape
    mem_spec = pl.BlockSpec((None, mk.shape[1], MEM_HD), lambda b: (b, 0, 0))
    return pl.pallas_call(
        _mem_attn_sample_kernel,
        grid=(n_seq,),
        in_specs=[pl.BlockSpec((None, ds, MEM_W), lambda b: (b, 0, OFF_MQ // MEM_W)), mem_spec, mem_spec],
        out_specs=pl.BlockSpec((None, ds, MEM_W), lambda b: (b, 0, 0)),
        out_shape=jax.ShapeDtypeStruct((n_seq, ds, MEM_W), bf16),
        compiler_params=_cp("parallel"),
        name="mem_attn_sample",
    )(z3, mk, mv)


def _mix_kernel(orr_ref, od_ref, om_ref, ga_ref, gb_ref, gc_ref, wr_ref, wd_ref, wm_ref, o_ref):
    mixed = (jax.nn.sigmoid(ga_ref[...]) * _dot(orr_ref[...], wr_ref[...])
             + jax.nn.sigmoid(gb_ref[...]) * _dot(od_ref[...], wd_ref[...])
             + jax.nn.sigmoid(gc_ref[...]) * _dot(om_ref[...], wm_ref[...]))
    o_ref[...] = mixed.astype(bf16)


def gated_mix(o_r, o_d, o_m, z, w_r, w_d, w_m, *, tm=1024, tn=512):
    t = o_r.shape[0]
    tm = min(tm, t)
    assert t % tm == 0 and D_MODEL % tn == 0
    gate = lambda off: pl.BlockSpec((tm, tn), lambda i, j: (i, off // tn + j))
    act = lambda w: pl.BlockSpec((tm, w), lambda i, j: (i, 0))
    wgt = lambda w: pl.BlockSpec((w, tn), lambda i, j: (0, j))
    return pl.pallas_call(
        _mix_kernel,
        grid=(t // tm, D_MODEL // tn),
        in_specs=[act(RET_V_W), act(DIFF_V_W), act(MEM_W), gate(OFF_GA), gate(OFF_GB), gate(OFF_GC),
                  wgt(RET_V_W), wgt(DIFF_V_W), wgt(MEM_W)],
        out_specs=pl.BlockSpec((tm, tn), lambda i, j: (i, j)),
        out_shape=jax.ShapeDtypeStruct((t, D_MODEL), bf16),
        compiler_params=_cp("parallel", "parallel"),
        name="gated_mix",
    )(o_r, o_d, o_m, z, z, z, w_r, w_d, w_m)


def _matmul_residual_kernel(x_ref, a_ref, w_ref, o_ref):
    o_ref[...] = x_ref[...] + _dot(a_ref[...], w_ref[...])


def matmul_residual(x, a, w, *, tm=1024, tn=512):
    t, k = a.shape
    n = w.shape[1]
    tm = min(tm, t)
    assert t % tm == 0 and n % tn == 0
    return pl.pallas_call(
        _matmul_residual_kernel,
        grid=(t // tm, n // tn),
        in_specs=[
            pl.BlockSpec((tm, tn), lambda i, j: (i, j)),
            pl.BlockSpec((tm, k), lambda i, j: (i, 0)),
            pl.BlockSpec((k, tn), lambda i, j: (0, j)),
        ],
        out_specs=pl.BlockSpec((tm, tn), lambda i, j: (i, j)),
        out_shape=jax.ShapeDtypeStruct((t, n), f32),
        compiler_params=_cp("parallel", "parallel"),
        name="matmul_residual",
    )(x, a, w)


_PEER_NTOP = PEER_TOPK + 1
_PEER_CAND = [(a, b) for a in range(_PEER_NTOP) for b in range(_PEER_NTOP) if (a + 1) * (b + 1) <= _PEER_NTOP]


def _top_values(x, k):
    out = []
    for i in range(k):
        mx = jnp.max(x, axis=0, keepdims=True)
        out.append(mx)
        if i + 1 < k:
            x = jnp.where(x == mx, -jnp.inf, x)
    return out


def _peer_select_kernel(pq_ref, keys_ref, thr_ref, a2_ref, e1_ref, e2_ref):
    for h in range(PEER_HEADS):
        sc = []
        for c in range(2):
            off = (2 * h + c) * PEER_DK
            sc.append(_dot_nt(keys_ref[h, c], pq_ref[:, off:off + PEER_DK]))
        t1 = _top_values(sc[0], _PEER_NTOP)
        t2 = _top_values(sc[1], _PEER_NTOP)
        cand = jnp.concatenate([t1[a] + t2[b] for a, b in _PEER_CAND], axis=0)
        top = _top_values(cand, _PEER_NTOP)
        m = top[0]
        zsum = jnp.ones_like(m)
        for r in range(1, PEER_TOPK):
            zsum = zsum + jnp.exp(top[r] - m)
        tau = 0.5 * (top[PEER_TOPK - 1] + top[PEER_TOPK])
        thr_ref[h] = tau - sc[0]
        a2_ref[h] = sc[1]
        e1_ref[h] = jnp.exp(sc[0] - t1[0]) / zsum
        e2_ref[h] = jnp.exp(sc[1] - t2[0])


def peer_select(pq, keys, *, tt=256):
    t = pq.shape[0]
    tt = min(tt, t)
    assert t % tt == 0
    big = pl.BlockSpec((PEER_HEADS, PEER_NKEYS, tt), lambda i: (0, 0, i))
    shp = jax.ShapeDtypeStruct((PEER_HEADS, PEER_NKEYS, t), f32)
    return pl.pallas_call(
        _peer_select_kernel,
        grid=(t // tt,),
        in_specs=[
            pl.BlockSpec((tt, PEER_HEADS * 2 * PEER_DK), lambda i: (i, 0)),
            pl.BlockSpec((PEER_HEADS, 2, PEER_NKEYS, PEER_DK), lambda i: (0, 0, 0, 0)),
        ],
        out_specs=[big, big, big, big],
        out_shape=[shp, shp, shp, shp],
        compiler_params=_cp("parallel"),
        name="peer_select",
    )(pq, keys)


PEER_ROWS = 8


def _gelu(x):
    return 0.5 * x * (1.0 + lax.erf(x * (2.0 ** -0.5)))


PEER_SUB = 16


def _peer_weights_block(c0, i0, ht_ref, wt_ref, thr_ref, e1_ref, a2_ref, e2_ref):
    cs = slice(c0, c0 + 128)
    isl = slice(i0, i0 + PEER_SUB)
    g = [None] * PEER_ROWS
    for h in range(PEER_HEADS):
        a2 = a2_ref[h, isl, cs]
        e2 = e2_ref[h, isl, cs]
        for r in range(PEER_ROWS):
            w = e1_ref[h, r:r + 1, cs] * e2
            sel = jnp.where(a2 >= thr_ref[h, r:r + 1, cs], w, 0.0)
            g[r] = sel if g[r] is None else g[r] + sel
    out = []
    for r in range(PEER_ROWS):
        rs = slice(r * PEER_NKEYS + i0, r * PEER_NKEYS + i0 + PEER_SUB)
        out.append((rs, cs, (g[r] * _gelu(ht_ref[rs, cs])).astype(bf16)))

    def store():
        for rs, cs_, val in out:
            wt_ref[rs, cs_] = val
    return store


def _spread(n_items, n_slots):
    return [range(-(-s * n_items // n_slots), -(-(s + 1) * n_items // n_slots)) for s in range(n_slots)]


def _peer_dense_kernel(xnt_ref, u_ref, vt_ref, thr_ref, e1_ref, a2_ref, e2_ref, o_ref,
                       ht0_ref, ht1_ref, wt0_ref, wt1_ref, acc_ref, *, nj):
    s = pl.program_id(0)
    j_out = lax.rem(s + (nj - 2), nj)

    @pl.when(s == 0)
    def _():
        for ref in (ht0_ref, ht1_ref, wt0_ref, wt1_ref, acc_ref):
            ref[...] = jnp.zeros_like(ref)

    @pl.when((s >= 2) & (j_out == 0))
    def _():
        acc_ref[...] = jnp.zeros_like(acc_ref)

    ec, tt = ht0_ref.shape
    d = acc_ref.shape[0]
    mt = 128
    nt = min(256, tt)

    def step(ht_w, ht_r, wt_w, wt_r):
        def mm_scores(m0, n0):
            val = _dot(u_ref[m0:m0 + mt, :], xnt_ref[:, n0:n0 + nt])

            def store():
                ht_w[m0:m0 + mt, n0:n0 + nt] = val
            return store

        def mm_out(m0, n0):
            val = acc_ref[m0:m0 + mt, n0:n0 + nt] + _dot(vt_ref[m0:m0 + mt, :], wt_r[:, n0:n0 + nt])

            def store():
                acc_ref[m0:m0 + mt, n0:n0 + nt] = val
            return store

        pieces = [(mm_scores, m0, n0) for m0 in range(0, ec, mt) for n0 in range(0, tt, nt)]
        pieces += [(mm_out, m0, n0) for m0 in range(0, d, mt) for n0 in range(0, tt, nt)]
        n1 = (ec // mt) * (tt // nt)
        pieces = [p for _, _, p in sorted(
            ((k + 0.5) / n1 if k < n1 else (k - n1 + 0.5) / (len(pieces) - n1), k, p)
            for k, p in enumerate(pieces))]
        blocks = [(c0, i0) for c0 in range(0, tt, 128) for i0 in range(0, PEER_NKEYS, PEER_SUB)]
        for (c0, i0), todo in zip(blocks, _spread(len(pieces), len(blocks))):
            stores = [pieces[k][0](*pieces[k][1:]) for k in todo]
            stores.append(_peer_weights_block(c0, i0, ht_r, wt_w, thr_ref, e1_ref, a2_ref, e2_ref))
            for store in stores:
                store()

    even = lax.rem(s, 2) == 0

    @pl.when(even)
    def _():
        step(ht0_ref, ht1_ref, wt1_ref, wt0_ref)

    @pl.when(jnp.logical_not(even))
    def _():
        step(ht1_ref, ht0_ref, wt0_ref, wt1_ref)

    @pl.when((s >= 2) & (j_out == nj - 1))
    def _():
        o_ref[...] = acc_ref[...].T


def peer_dense(xnt, u, vt, thr, a2, e1, e2, *, tt=512):
    d, t = xnt.shape
    n_exp = u.shape[0]
    tt = min(tt, t)
    ec = PEER_ROWS * PEER_NKEYS
    assert t % tt == 0 and n_exp % ec == 0 and tt % 128 == 0
    nj = n_exp // ec
    n = (t // tt) * nj
    assert nj > 2
    c_mm1 = lambda s: jnp.minimum(s, n - 1)
    c_wts = lambda s: jnp.clip(s - 1, 0, n - 1)
    c_mm2 = lambda s: jnp.clip(s - 2, 0, n - 1)
    once = pl.Buffered(1)
    row_blk = pl.BlockSpec((PEER_HEADS, PEER_ROWS, tt), lambda s: (0, c_wts(s) % nj, c_wts(s) // nj))
    all_blk = pl.BlockSpec((PEER_HEADS, PEER_NKEYS, tt), lambda s: (0, 0, c_wts(s) // nj), pipeline_mode=once)
    return pl.pallas_call(
        functools.partial(_peer_dense_kernel, nj=nj),
        grid=(n + 2,),
        in_specs=[
            pl.BlockSpec((d, tt), lambda s: (0, c_mm1(s) // nj), pipeline_mode=once),
            pl.BlockSpec((ec, d), lambda s: (c_mm1(s) % nj, 0)),
            pl.BlockSpec((d, ec), lambda s: (0, c_mm2(s) % nj)),
            row_blk, row_blk, all_blk, all_blk,
        ],
        out_specs=pl.BlockSpec((tt, d), lambda s: (c_mm2(s) // nj, 0)),
        out_shape=jax.ShapeDtypeStruct((t, d), f32),
        scratch_shapes=[pltpu.VMEM((ec, tt), f32), pltpu.VMEM((ec, tt), f32), pltpu.VMEM((ec, tt), bf16),
                        pltpu.VMEM((ec, tt), bf16), pltpu.VMEM((d, tt), f32)],
        compiler_params=_cp("arbitrary"),
        name="peer_dense",
    )(xnt, u, vt, thr, e1, a2, e2)


def _final_norm_kernel(h_ref, p_ref, g_ref, o_ref):
    o_ref[...] = _rms_rows(h_ref[...] + p_ref[...], g_ref[...])


def final_norm(h, p, g, *, tm=256):
    t, d = h.shape
    tm = min(tm, t)
    assert t % tm == 0
    blk = pl.BlockSpec((tm, d), lambda i: (i, 0))
    return pl.pallas_call(
        _final_norm_kernel,
        grid=(t // tm,),
        in_specs=[blk, blk, pl.BlockSpec((1, d), lambda i: (0, 0))],
        out_specs=blk,
        out_shape=jax.ShapeDtypeStruct((t, d), f32),
        compiler_params=_cp("parallel"),
        name="final_norm",
    )(h, p, g.reshape(1, d))


def _token_tail(x2, z, o_r, o_d, o_m, wts):
    mixed = gated_mix(o_r, o_d, o_m, z, wts["w_up_ret"], wts["w_up_diff"], wts["w_up_mem"])
    h = matmul_residual(x2, mixed, wts["w_out"])
    pq, xn = norm_matmul(h, wts["norm_ffn_w"], wts["peer_w_q"], with_rows=True)
    thr, a2, e1, e2 = peer_select(pq, wts["peer_keys"])
    peer = peer_dense(xn.T, wts["peer_u"], wts["peer_vt"], thr, a2, e1, e2)
    return final_norm(h, peer, wts["norm_final_w"])


def kernel(x_prompt, x_sample, mem_prompt, state_ret, cache_diff_k, cache_diff_v, cache_mem_k, cache_mem_v, page_table, norm_mix_w, norm_mem_w, norm_ffn_w, norm_final_w, w_in, w_mem_k, w_mem_v, diff_lambda_q1, diff_lambda_k1, diff_lambda_q2, diff_lambda_k2, diff_subln_w, rel_bias, w_up_ret, w_up_diff, w_up_mem, w_out, peer_w_q, peer_keys, peer_u, peer_v):
    b, s, d = x_prompt.shape
    db, ds, _ = x_sample.shape
    n_pages = page_table.shape[1]
    page = cache_diff_k.shape[2]
    past = n_pages * page
    n_mem = mem_prompt.shape[1]
    assert w_in.shape[0] == 1 and d == D_MODEL, "single-layer trunk"
    l = 0
    lam_init = _lambda_init(l)

    wts = {
        "w_up_ret": w_up_ret[l].astype(bf16), "w_up_diff": w_up_diff[l].astype(bf16),
        "w_up_mem": w_up_mem[l].astype(bf16), "w_out": w_out[l].astype(bf16),
        "norm_ffn_w": norm_ffn_w[l], "peer_w_q": peer_w_q[l].astype(bf16), "peer_keys": peer_keys[l],
        "peer_u": peer_u[l].astype(bf16), "peer_vt": peer_v[l].astype(bf16).T, "norm_final_w": norm_final_w,
    }
    w_in_b = w_in[l].astype(bf16)
    lam = diff_lambda(diff_lambda_q1[l], diff_lambda_k1[l], diff_lambda_q2[l], diff_lambda_k2[l], lam_init)

    xp = x_prompt.reshape(b * s, d)
    zp = norm_matmul(xp, norm_mix_w[l], w_in_b)
    w_mem = jnp.concatenate([w_mem_k[l], w_mem_v[l]], axis=1).astype(bf16)
    mkv = norm_matmul(mem_prompt.reshape(b * n_mem, d), norm_mem_w[l], w_mem)
    o_r, st_p = retention_prompt(zp, b, s)
    o_d = diff_attn_prompt(zp, lam, rel_bias, diff_subln_w[l], b, s, lam_init)
    o_m = mem_attn_prompt(zp, mkv, b, s, n_mem)
    y_prompt = _token_tail(xp, zp, o_r, o_d, o_m, wts).reshape(b, s, d)

    xs = x_sample.reshape(db * ds, d)
    zs = norm_matmul(xs, norm_mix_w[l], w_in_b)
    zs3 = zs.reshape(db, ds, IN_WIDTH)
    o_r_s, st_s = retention_sample(zs, state_ret[l], db, ds, past)
    ck = jnp.transpose(cache_diff_k[l], (0, 2, 3, 1))
    cv = cache_diff_v[l].reshape(-1, page * DIFF_HEADS, 2 * DIFF_HD)
    o_d_s = diff_attn_sample(zs3, ck, cv, page_table, lam, rel_bias, diff_subln_w[l], lam_init)
    o_m_s = mem_attn_sample(zs3, cache_mem_k[l].reshape(db, n_mem * MEM_HEADS, MEM_HD),
                            cache_mem_v[l].reshape(db, n_mem * MEM_HEADS, MEM_HD))
    y_sample = _token_tail(xs, zs, o_r_s, o_d_s.reshape(db * ds, DIFF_V_W), o_m_s.reshape(db * ds, MEM_W),
                           wts).reshape(db, ds, d)

    nh2 = 2 * DIFF_HEADS
    return (
        y_prompt,
        y_sample,
        st_p[None],
        zp[:, OFF_DK:OFF_DK + DIFF_QK_W].reshape(1, b, s, nh2, DIFF_HD),
        zp[:, OFF_DV:OFF_DV + DIFF_V_W].reshape(1, b, s, DIFF_HEADS, 2 * DIFF_HD),
        mkv[:, :MEM_W].reshape(1, b, n_mem, MEM_HEADS, MEM_HD),
        mkv[:, MEM_W:].reshape(1, b, n_mem, MEM_HEADS, MEM_HD),
        st_s[None],
        zs[:, OFF_DK:OFF_DK + DIFF_QK_W].reshape(1, db, ds, nh2, DIFF_HD),
        zs[:, OFF_DV:OFF_DV + DIFF_V_W].reshape(1, db, ds, DIFF_HEADS, 2 * DIFF_HD),
    )
```
